```python
import math
import jax, jax.numpy as jnp
from jax import lax
import numpy as np

D_MODEL = 1024
BATCH = 8
SEQ = 8192
DEPTH = 1

D_MIX = D_MODEL
DIFF_WIDTH = D_MIX // 2
GLA_WIDTH = D_MIX - DIFF_WIDTH
DIFF_HEADS = 4
DIFF_V_DIM = DIFF_WIDTH // DIFF_HEADS
DIFF_QK_DIM = DIFF_V_DIM // 2
GLA_HEADS = 4
GLA_V_DIM = GLA_WIDTH // GLA_HEADS
GLA_K_DIM = GLA_V_DIM // 2
GLA_GATE_RANK = 16
GLA_GATE_NORMALIZER = 16.0
GLA_CHUNK = 64
Q_BLOCK = 128
EPS = 1e-6

DIFF_QK_COLS = DIFF_HEADS * 2 * DIFF_QK_DIM
GLA_QK_COLS = GLA_HEADS * GLA_K_DIM
IN_SIZES = (
    DIFF_QK_COLS,
    DIFF_QK_COLS,
    DIFF_WIDTH,
    DIFF_WIDTH,
    GLA_QK_COLS,
    GLA_QK_COLS,
    GLA_WIDTH,
    GLA_WIDTH,
    GLA_GATE_RANK,
)
IN_COLS = sum(IN_SIZES)

kernel_name = "hymba_diffattn_gla_hybrid"


def rms_norm(x, w):
    xf = x.astype(jnp.float32)
    y = xf * lax.rsqrt(jnp.mean(xf * xf, axis=-1, keepdims=True) + EPS)
    return (y * w.astype(jnp.float32)).astype(x.dtype)


def lambda_init(layer_idx):
    return 0.8 - 0.6 * math.exp(-0.3 * layer_idx)


def alibi_slopes(n_heads):
    return jnp.exp2(-8.0 * jnp.arange(1, n_heads + 1, dtype=jnp.float32) / n_heads)


def diff_attention(q, k, v, lam, slopes):
    B, H, _, T, d = q.shape
    nb = T // Q_BLOCK
    scale = d ** -0.5
    qf = q.astype(jnp.float32) * scale
    kf = k.astype(jnp.float32)
    vf = v.astype(jnp.float32)
    q_blocks = jnp.moveaxis(qf.reshape(B, H, 2, nb, Q_BLOCK, d), 3, 0)
    key_pos = jnp.arange(T, dtype=jnp.int32)

    def one_block(args):
        q_blk, blk = args
        q_pos = blk * Q_BLOCK + jnp.arange(Q_BLOCK, dtype=jnp.int32)
        dist = q_pos[:, None] - key_pos[None, :]
        bias = -slopes[:, None, None] * dist.astype(jnp.float32)
        s = jnp.einsum('bhiqd,bhikd->bhiqk', q_blk, kf) + bias[None, :, None]
        s = jnp.where(dist[None, None, None] >= 0, s, -jnp.inf)
        p = jax.nn.softmax(s, axis=-1)
        p_diff = p[:, :, 0] - lam * p[:, :, 1]
        return jnp.einsum('bhqk,bhkd->bhqd', p_diff, vf)

    out = lax.map(one_block, (q_blocks, jnp.arange(nb, dtype=jnp.int32)))
    return jnp.moveaxis(out, 0, 2).reshape(B, H, T, v.shape[-1]).astype(v.dtype)


def gla_chunked(q, k, v, g):
    B, H, T, dk = q.shape
    dv = v.shape[-1]
    C = GLA_CHUNK
    N = T // C
    qf = (q.astype(jnp.float32) * dk ** -0.5).reshape(B, H, N, C, dk)
    kf = k.astype(jnp.float32).reshape(B, H, N, C, dk)
    vf = v.astype(jnp.float32).reshape(B, H, N, C, dv)
    b = jnp.cumsum(g.astype(jnp.float32).reshape(B, H, N, C, dk), axis=3)
    b_last = b[:, :, :, -1:]
    q_in = qf * jnp.exp(b)
    k_in = kf * jnp.exp(-b)
    k_state = kf * jnp.exp(b_last - b)
    causal = jnp.tril(jnp.ones((C, C), dtype=bool))
    a = jnp.einsum('bhncd,bhnsd->bhncs', q_in, k_in)
    a = jnp.where(causal, a, 0.0)
    o_intra = jnp.einsum('bhncs,bhnse->bhnce', a, vf)
    kv = jnp.einsum('bhncd,bhnce->bhnde', k_state, vf)
    decay = jnp.exp(b_last[:, :, :, 0])

    def step(state, inp):
        kv_n, dec_n = inp
        return state * dec_n[..., None] + kv_n, state

    s0 = jnp.zeros((B, H, dk, dv), jnp.float32)
    _, s_prev = lax.scan(step, s0, (jnp.moveaxis(kv, 2, 0), jnp.moveaxis(decay, 2, 0)))
    s_prev = jnp.moveaxis(s_prev, 0, 2)
    o_inter = jnp.einsum('bhncd,bhnde->bhnce', q_in, s_prev)
    return (o_intra + o_inter).reshape(B, H, T, dv).astype(v.dtype)


def hybrid_layer(x, norm_w, w_in, w_gate_up, b_gate, lq1, lk1, lq2, lk2,
                 diff_subln_w, gla_norm_w, w_out, layer_idx):
    B, T, _ = x.shape
    h = rms_norm(x, norm_w)
    proj = jnp.einsum('btd,de->bte', h, w_in)
    split_points = []
    acc = 0
    for s in IN_SIZES[:-1]:
        acc += s
        split_points.append(acc)
    dq, dk, dv, dg, gq, gk, gv, gg, ga = jnp.split(proj, split_points, axis=-1)

    q_a = dq.reshape(B, T, DIFF_HEADS, 2, DIFF_QK_DIM).transpose(0, 2, 3, 1, 4)
    k_a = dk.reshape(B, T, DIFF_HEADS, 2, DIFF_QK_DIM).transpose(0, 2, 3, 1, 4)
    v_a = dv.reshape(B, T, DIFF_HEADS, DIFF_V_DIM).transpose(0, 2, 1, 3)
    lam_init = lambda_init(layer_idx)
    lam = (jnp.exp(jnp.sum(lq1.astype(jnp.float32) * lk1.astype(jnp.float32)))
           - jnp.exp(jnp.sum(lq2.astype(jnp.float32) * lk2.astype(jnp.float32)))
           + lam_init)
    o_a = diff_attention(q_a, k_a, v_a, lam, alibi_slopes(DIFF_HEADS))
    o_a = rms_norm(o_a, diff_subln_w) * (1.0 - lam_init)
    o_a = o_a.transpose(0, 2, 1, 3).reshape(B, T, DIFF_WIDTH) * jax.nn.silu(dg)

    q_b = gq.reshape(B, T, GLA_HEADS, GLA_K_DIM).transpose(0, 2, 1, 3)
    k_b = gk.reshape(B, T, GLA_HEADS, GLA_K_DIM).transpose(0, 2, 1, 3)
    v_b = gv.reshape(B, T, GLA_HEADS, GLA_V_DIM).transpose(0, 2, 1, 3)
    g_logit = jnp.einsum('btr,rk->btk', ga, w_gate_up) + b_gate
    g_log = jax.nn.log_sigmoid(g_logit.astype(jnp.float32)) / GLA_GATE_NORMALIZER
    g_b = g_log.reshape(B, T, GLA_HEADS, GLA_K_DIM).transpose(0, 2, 1, 3)
    o_b = gla_chunked(q_b, k_b, v_b, g_b)
    o_b = rms_norm(o_b, gla_norm_w)
    o_b = o_b.transpose(0, 2, 1, 3).reshape(B, T, GLA_WIDTH) * jax.nn.silu(gg)

    y = jnp.concatenate([o_a, o_b], axis=-1)
    return x + jnp.einsum('btm,md->btd', y, w_out)


def setup_inputs(seed: int = 0) -> dict:
    key = jax.random.key(seed)
    ks = jax.random.split(key, 16)
    f32 = jnp.float32
    x = jax.random.normal(ks[0], (BATCH, SEQ, D_MODEL), f32)
    norm_w = 1.0 + 0.01 * jax.random.normal(ks[1], (DEPTH, D_MODEL), f32)
    w_in = jax.random.normal(ks[2], (DEPTH, D_MODEL, IN_COLS), f32) * D_MODEL ** -0.5
    w_gate_up = jax.random.normal(ks[3], (DEPTH, GLA_GATE_RANK, GLA_QK_COLS), f32) * GLA_GATE_RANK ** -0.5
    b_gate = 0.1 * jax.random.normal(ks[4], (DEPTH, GLA_QK_COLS), f32)
    lambda_q1 = 0.1 * jax.random.normal(ks[5], (DEPTH, DIFF_QK_DIM), f32)
    lambda_k1 = 0.1 * jax.random.normal(ks[6], (DEPTH, DIFF_QK_DIM), f32)
    lambda_q2 = 0.1 * jax.random.normal(ks[7], (DEPTH, DIFF_QK_DIM), f32)
    lambda_k2 = 0.1 * jax.random.normal(ks[8], (DEPTH, DIFF_QK_DIM), f32)
    diff_subln_w = 1.0 + 0.01 * jax.random.normal(ks[9], (DEPTH, DIFF_V_DIM), f32)
    gla_norm_w = 1.0 + 0.01 * jax.random.normal(ks[10], (DEPTH, GLA_V_DIM), f32)
    w_out = jax.random.normal(ks[11], (DEPTH, D_MIX, D_MODEL), f32) * D_MIX ** -0.5
    final_norm_w = 1.0 + 0.01 * jax.random.normal(ks[12], (D_MODEL,), f32)
    return {"x": x, "norm_w": norm_w, "w_in": w_in, "w_gate_up": w_gate_up,
            "b_gate": b_gate, "lambda_q1": lambda_q1, "lambda_k1": lambda_k1,
            "lambda_q2": lambda_q2, "lambda_k2": lambda_k2,
            "diff_subln_w": diff_subln_w, "gla_norm_w": gla_norm_w,
            "w_out": w_out, "final_norm_w": final_norm_w}


def reference(x, norm_w, w_in, w_gate_up, b_gate, lambda_q1, lambda_k1, lambda_q2,
              lambda_k2, diff_subln_w, gla_norm_w, w_out, final_norm_w):
    h = x
    for layer in range(DEPTH):
        h = hybrid_layer(h, norm_w[layer], w_in[layer], w_gate_up[layer], b_gate[layer],
                         lambda_q1[layer], lambda_k1[layer], lambda_q2[layer], lambda_k2[layer],
                         diff_subln_w[layer], gla_norm_w[layer], w_out[layer], layer)
    return rms_norm(h, final_norm_w)
```

```python
import functools
import math

import numpy as np
import jax
import jax.numpy as jnp
from jax import lax
from jax.experimental import pallas as pl
from jax.experimental.pallas import tpu as pltpu

EPS = 1e-6
DIFF_HEADS = 4
GLA_HEADS = 4
GLA_GATE_NORMALIZER = 16.0
GLA_CHUNK = 64
LOG2E = math.log2(math.e)

LANES_V7X = 128
VMEM_BYTES_V7X = 64 * 1024 * 1024
VMEM_LIMIT_V7X = 56 * 1024 * 1024

F32 = jnp.float32
BF16 = jnp.bfloat16
NT_DIMS = (((1,), (1,)), ((), ()))
TN_DIMS = (((0,), (0,)), ((), ()))

N_SPLIT = 3


def _lambda_init(layer_idx):
    return 0.8 - 0.6 * math.exp(-0.3 * layer_idx)


def _alibi_feature_consts(n_heads):
    slopes = np.exp2(-8.0 * np.arange(1, n_heads + 1, dtype=np.float64) / n_heads)
    fq = np.zeros((n_heads, 1, LANES_V7X), np.float32)
    fk = np.zeros((n_heads, 1, LANES_V7X), np.float32)
    for h in range(n_heads):
        rem = np.float32(slopes[h] * LOG2E)
        pieces = []
        for _ in range(N_SPLIT):
            p = np.float32(rem).astype(BF16).astype(np.float32)
            pieces.append(p)
            rem = np.float32(rem - p)
        for x, p in enumerate(pieces):
            fk[h, 0, x] = -p
            fk[h, 0, N_SPLIT + x] = -p
            fq[h, 0, 2 * N_SPLIT + x] = p
            fq[h, 0, 3 * N_SPLIT + x] = p
    return jnp.asarray(fq), jnp.asarray(fk)


def _pos_features(pos, lane, first):
    hi = (pos >> 8) << 8
    lo = pos & 255
    in_hi = (lane >= first) & (lane < first + N_SPLIT)
    in_lo = (lane >= first + N_SPLIT) & (lane < first + 2 * N_SPLIT)
    return jnp.where(in_hi, hi, jnp.where(in_lo, lo, 0)).astype(F32)


def _inproj_kernel(x_ref, nw_ref, wa_ref, wg_ref, wr_ref, oa_ref, og_ref, or_ref, *, q_cols, q_scale):
    x = x_ref[...]
    ms = jnp.mean(x * x, axis=-1, keepdims=True)
    h = (x * lax.rsqrt(ms + EPS) * nw_ref[...]).astype(BF16)
    acc = jnp.dot(h, wa_ref[...], preferred_element_type=F32)
    oa_ref[:, :q_cols] = (acc[:, :q_cols] * q_scale).astype(BF16)
    oa_ref[:, q_cols:] = acc[:, q_cols:].astype(BF16)
    og_ref[...] = jnp.dot(h, wg_ref[...], preferred_element_type=F32)
    or_ref[...] = jnp.dot(h, wr_ref[...], preferred_element_type=F32).astype(BF16)


def _inproj(x2, nw, wa, wg, wr, *, q_cols, q_scale, tm):
    n, d = x2.shape
    na, ng, nr = wa.shape[1], wg.shape[1], wr.shape[1]
    kern = functools.partial(_inproj_kernel, q_cols=q_cols, q_scale=q_scale)
    return pl.pallas_call(
        kern,
        grid=(n // tm,),
        in_specs=[
            pl.BlockSpec((tm, d), lambda i: (i, 0)),
            pl.BlockSpec((1, d), lambda i: (0, 0)),
            pl.BlockSpec((d, na), lambda i: (0, 0)),
            pl.BlockSpec((d, ng), lambda i: (0, 0)),
            pl.BlockSpec((d, nr), lambda i: (0, 0)),
        ],
        out_specs=[
            pl.BlockSpec((tm, na), lambda i: (i, 0)),
            pl.BlockSpec((tm, ng), lambda i: (i, 0)),
            pl.BlockSpec((tm, nr), lambda i: (i, 0)),
        ],
        out_shape=[
            jax.ShapeDtypeStruct((n, na), BF16),
            jax.ShapeDtypeStruct((n, ng), F32),
            jax.ShapeDtypeStruct((n, nr), BF16),
        ],
        compiler_params=pltpu.CompilerParams(
            dimension_semantics=("arbitrary",), vmem_limit_bytes=VMEM_LIMIT_V7X),
        name="inproj",
    )(x2, nw, wa, wg, wr)


def _attn_kernel(q_ref, k_ref, v_ref, dg_ref, lq1_ref, lk1_ref, lq2_ref, lk2_ref, sw_ref,
                 fqc_ref, fkc_ref, o_ref, kf_ref, qp_ref, m_ref, acc_ref,
                 *, seq, tq, tk, lam_init):
    dh = LANES_V7X
    half = dh // 2
    n_q = seq // tq
    n_k = seq // tk

    fkc = fkc_ref[...]
    lane_k = lax.broadcasted_iota(jnp.int32, (tk, dh), 1)
    row_k = lax.broadcasted_iota(jnp.int32, (tk, dh), 0)

    def build_kf(j, carry):
        start = pl.multiple_of(j * tk, tk)
        f = _pos_features(row_k + start, lane_k, 2 * N_SPLIT) + fkc
        kf_ref[pl.ds(start, tk), :] = f.astype(BF16)
        return carry

    lax.fori_loop(0, n_k, build_kf, 0)

    lam = (jnp.exp(jnp.sum(lq1_ref[...] * lk1_ref[...], axis=-1, keepdims=True))
           - jnp.exp(jnp.sum(lq2_ref[...] * lk2_ref[...], axis=-1, keepdims=True))
           + lam_init)

    lane_q = lax.broadcasted_iota(jnp.int32, (tq, dh), 1)
    row_q = lax.broadcasted_iota(jnp.int32, (tq, dh), 0)
    fqc = fqc_ref[...]
    ones_blk = jnp.where(lax.broadcasted_iota(jnp.int32, (tk, dh), 1) == 0, 1.0, 0.0).astype(BF16)
    r2 = lax.broadcasted_iota(jnp.int32, (2 * tq, tk), 0)
    c2 = lax.broadcasted_iota(jnp.int32, (2 * tq, tk), 1)
    col_minus_row = c2 - jnp.where(r2 >= tq, r2 - tq, r2)

    def kv_step(j, qstart, masked):
        kstart = pl.multiple_of(j * tk, tk)
        kp = jnp.concatenate([k_ref[pl.ds(kstart, tk), :], kf_ref[pl.ds(kstart, tk), :]], axis=1)
        s = lax.dot_general(qp_ref[...], kp, NT_DIMS, preferred_element_type=F32)
        if masked:
            s = jnp.where(col_minus_row <= qstart - kstart, s, -jnp.inf)
        m_old = m_ref[...]
        m_new = jnp.maximum(m_old, jnp.max(s, axis=1, keepdims=True))
        alpha = jnp.exp2(m_old - m_new)
        p = jnp.exp2(s - m_new).astype(BF16)
        vp = jnp.concatenate([v_ref[pl.ds(kstart, tk), :], ones_blk], axis=1)
        pv = jnp.dot(p, vp, preferred_element_type=F32)
        acc_ref[...] = alpha * acc_ref[...] + pv
        m_ref[...] = m_new

    def q_block(qi, carry):
        qstart = pl.multiple_of(qi * tq, tq)
        q = q_ref[pl.ds(qstart, tq), :]
        zero = jnp.zeros_like(q)
        fq = (_pos_features(row_q + qstart, lane_q, 0) + fqc).astype(BF16)
        qp_ref[0:tq, 0:dh] = jnp.where(lane_q < half, q, zero)
        qp_ref[tq:2 * tq, 0:dh] = jnp.where(lane_q >= half, q, zero)
        qp_ref[0:tq, dh:2 * dh] = fq
        qp_ref[tq:2 * tq, dh:2 * dh] = fq
        m_ref[...] = jnp.full(m_ref.shape, -jnp.inf, F32)
        acc_ref[...] = jnp.zeros(acc_ref.shape, F32)

        n_full = qstart // tk

        def full_step(j, c):
            kv_step(j, qstart, masked=False)
            return c

        lax.fori_loop(0, n_full, full_step, 0)
        kv_step(n_full, qstart, masked=True)

        acc = acc_ref[...]
        o0 = acc[0:tq, 0:dh]
        o1 = acc[tq:2 * tq, 0:dh]
        l0 = acc[0:tq, dh:dh + 1]
        l1 = acc[tq:2 * tq, dh:dh + 1]
        o = o0 * (1.0 / l0) - lam * (o1 * (1.0 / l1))
        ms = jnp.mean(o * o, axis=-1, keepdims=True)
        y = (o * lax.rsqrt(ms + EPS) * sw_ref[...]) * (1.0 - lam_init)
        g = dg_ref[pl.ds(qstart, tq), :]
        y = y * (g * (1.0 / (1.0 + jnp.exp(-g))))
        o_ref[pl.ds(qstart, tq), :] = y.astype(o_ref.dtype)
        return carry

    lax.fori_loop(0, n_q, q_block, 0)


def _diff_attention(pa, pg, lq1, lk1, lq2, lk2, sw, fqc, fkc, *, lam_init, tq, tk):
    b, seq, _ = pa.shape
    dh = LANES_V7X
    h = DIFF_HEADS
    assert tk % tq == 0 and seq % tk == 0 and tq <= 256
    kern = functools.partial(_attn_kernel, seq=seq, tq=tq, tk=tk, lam_init=lam_init)
    vec = lambda n: pl.BlockSpec((1, n), lambda i, j: (0, 0))
    return pl.pallas_call(
        kern,
        grid=(b, h),
        in_specs=[
            pl.BlockSpec((None, seq, dh), lambda i, j: (i, 0, j)),
            pl.BlockSpec((None, seq, dh), lambda i, j: (i, 0, h + j)),
            pl.BlockSpec((None, seq, dh), lambda i, j: (i, 0, 2 * h + j)),
            pl.BlockSpec((None, seq, dh), lambda i, j: (i, 0, j)),
            vec(lq1.shape[1]), vec(lk1.shape[1]), vec(lq2.shape[1]), vec(lk2.shape[1]),
            vec(dh),
            pl.BlockSpec((None, 1, dh), lambda i, j: (j, 0, 0)),
            pl.BlockSpec((None, 1, dh), lambda i, j: (j, 0, 0)),
        ],
        out_specs=pl.BlockSpec((None, seq, dh), lambda i, j: (i, 0, j)),
        out_shape=jax.ShapeDtypeStruct((b, seq, h * dh), BF16),
        scratch_shapes=[
            pltpu.VMEM((seq, dh), BF16),
            pltpu.VMEM((2 * tq, 2 * dh), BF16),
            pltpu.VMEM((2 * tq, 1), F32),
            pltpu.VMEM((2 * tq, 2 * dh), F32),
        ],
        compiler_params=pltpu.CompilerParams(
            dimension_semantics=("arbitrary", "arbitrary"), vmem_limit_bytes=VMEM_LIMIT_V7X),
        name="diff_attn",
    )(pa, pa, pa, pg, lq1, lk1, lq2, lk2, sw, fqc, fkc)


def _gla_kernel(q_ref, k_ref, v_ref, gg_ref, ga_ref, wgu_ref, bg_ref, nw_ref, o_ref, st_ref,
                *, lb, chunk, heads):
    dk = q_ref.shape[1] // heads
    dv = v_ref.shape[1] // heads
    n_chunks = lb // chunk

    @pl.when(pl.program_id(1) == 0)
    def _():
        st_ref[...] = jnp.zeros(st_ref.shape, F32)

    z = jnp.dot(ga_ref[...], wgu_ref[...], preferred_element_type=F32) + bg_ref[...]
    g = (jnp.minimum(z, 0.0) - jnp.log(1.0 + jnp.exp(-jnp.abs(z)))) * (1.0 / GLA_GATE_NORMALIZER)

    r = lax.broadcasted_iota(jnp.int32, (lb, lb), 0)
    c = lax.broadcasted_iota(jnp.int32, (lb, lb), 1)
    same_chunk = (r // chunk) == (c // chunk)
    causal = same_chunk & (c <= r)
    ltri = jnp.where(causal, 1.0, 0.0).astype(BF16)
    lfull = jnp.where(same_chunk, 1.0, 0.0).astype(BF16)
    g_hi = g.astype(BF16)
    g_lo = (g - g_hi.astype(F32)).astype(BF16)
    ghl = jnp.concatenate([g_hi, g_lo], axis=1)
    bb = jnp.dot(ltri, ghl, preferred_element_type=F32)
    bl = jnp.dot(lfull, ghl, preferred_element_type=F32)
    nk = heads * dk
    b_cum = bb[:, :nk] + bb[:, nk:]
    b_last = bl[:, :nk] + bl[:, nk:]

    q = q_ref[...].astype(F32) * (dk ** -0.5)
    k = k_ref[...].astype(F32)
    q_in = (q * jnp.exp(b_cum)).astype(BF16)
    k_in = (k * jnp.exp(-b_cum)).astype(BF16)
    k_st = (k * jnp.exp(b_last - b_cum)).astype(BF16)
    decay = jnp.exp(b_last)

    head_of_lane = lax.broadcasted_iota(jnp.int32, (lb, nk), 1) // dk
    zero_bf = jnp.zeros((lb, nk), BF16)

    o_intra = []
    k_st_h = []
    for h in range(heads):
        sel = head_of_lane == h
        a = lax.dot_general(jnp.where(sel, q_in, zero_bf), k_in, NT_DIMS, preferred_element_type=F32)
        a = jnp.where(causal, a, 0.0).astype(BF16)
        o_intra.append(jnp.dot(a, v_ref[:, h * dv:(h + 1) * dv], preferred_element_type=F32))
        k_st_h.append(jnp.where(sel, k_st, zero_bf))

    o_inter = []
    for ci in range(n_chunks):
        rows = slice(ci * chunk, (ci + 1) * chunk)
        st = st_ref[...]
        o_inter.append(lax.dot_general(q_in[rows], st.astype(BF16), NT_DIMS, preferred_element_type=F32))
        kv_t = [lax.dot_general(v_ref[rows, h * dv:(h + 1) * dv], k_st_h[h][rows], TN_DIMS,
                                preferred_element_type=F32) for h in range(heads)]
        st_ref[...] = st * decay[ci * chunk:ci * chunk + 1, :] + jnp.concatenate(kv_t, axis=0)
    o_inter = jnp.concatenate(o_inter, axis=0)

    for h in range(heads):
        cols = slice(h * dv, (h + 1) * dv)
        o = o_intra[h] + o_inter[:, cols]
        ms = jnp.mean(o * o, axis=-1, keepdims=True)
        y = o * lax.rsqrt(ms + EPS) * nw_ref[...]
        gate = gg_ref[:, cols]
        y = y * (gate * (1.0 / (1.0 + jnp.exp(-gate))))
        o_ref[:, cols] = y.astype(o_ref.dtype)


def _gla(pa, pg, pr, wgu, bg, nw, *, lb):
    b, seq, _ = pa.shape
    heads = GLA_HEADS
    nk = wgu.shape[1]
    nv = pg.shape[2] // 2
    a_q = 3 * DIFF_HEADS * LANES_V7X
    assert a_q % nk == 0 and (a_q + 2 * nk) % nv == 0 and seq % lb == 0 and lb % GLA_CHUNK == 0
    kern = functools.partial(_gla_kernel, lb=lb, chunk=GLA_CHUNK, heads=heads)
    return pl.pallas_call(
        kern,
        grid=(b, seq // lb),
        in_specs=[
            pl.BlockSpec((None, lb, nk), lambda i, t: (i, t, a_q // nk)),
            pl.BlockSpec((None, lb, nk), lambda i, t: (i, t, a_q // nk + 1)),
            pl.BlockSpec((None, lb, nv), lambda i, t: (i, t, (a_q + 2 * nk) // nv)),
            pl.BlockSpec((None, lb, nv), lambda i, t: (i, t, 1)),
            pl.BlockSpec((None, lb, pr.shape[2]), lambda i, t: (i, t, 0)),
            pl.BlockSpec(wgu.shape, lambda i, t: (0, 0)),
            pl.BlockSpec((1, nk), lambda i, t: (0, 0)),
            pl.BlockSpec((1, nv // heads), lambda i, t: (0, 0)),
        ],
        out_specs=pl.BlockSpec((None, lb, nv), lambda i, t: (i, t, 0)),
        out_shape=jax.ShapeDtypeStruct((b, seq, nv), BF16),
        scratch_shapes=[pltpu.VMEM((nv, nk), F32)],
        compiler_params=pltpu.CompilerParams(
            dimension_semantics=("arbitrary", "arbitrary"), vmem_limit_bytes=VMEM_LIMIT_V7X),
        name="gla",
    )(pa, pa, pa, pg, pr, wgu, bg, nw)


def _outproj_kernel(x_ref, ya_ref, yb_ref, w_ref, fw_ref, o_ref, *, final_norm):
    na = ya_ref.shape[1]
    mix = jnp.dot(ya_ref[...], w_ref[:na, :], preferred_element_type=F32)
    mix = mix + jnp.dot(yb_ref[...], w_ref[na:, :], preferred_element_type=F32)
    hres = x_ref[...] + mix
    if final_norm:
        ms = jnp.mean(hres * hres, axis=-1, keepdims=True)
        hres = hres * lax.rsqrt(ms + EPS) * fw_ref[...]
    o_ref[...] = hres


def _outproj(x2, ya, yb, w, fw, *, final_norm, tm):
    n, d = x2.shape
    na, nb = ya.shape[1], yb.shape[1]
    kern = functools.partial(_outproj_kernel, final_norm=final_norm)
    return pl.pallas_call(
        kern,
        grid=(n // tm,),
        in_specs=[
            pl.BlockSpec((tm, d), lambda i: (i, 0)),
            pl.BlockSpec((tm, na), lambda i: (i, 0)),
            pl.BlockSpec((tm, nb), lambda i: (i, 0)),
            pl.BlockSpec((na + nb, d), lambda i: (0, 0)),
            pl.BlockSpec((1, d), lambda i: (0, 0)),
        ],
        out_specs=pl.BlockSpec((tm, d), lambda i: (i, 0)),
        out_shape=jax.ShapeDtypeStruct((n, d), F32),
        compiler_params=pltpu.CompilerParams(
            dimension_semantics=("arbitrary",), vmem_limit_bytes=VMEM_LIMIT_V7X),
        name="outproj",
    )(x2, ya, yb, w, fw)


def _block_sizes(seq):
    return dict(tm_in=min(512, seq), tq=min(256, seq), tk=min(512, seq),
                lb=min(512, seq), tm_out=min(1024, seq))


def _layer(x, norm_w, w_in, w_gate_up, b_gate, lq1, lk1, lq2, lk2, subln_w, gla_norm_w, w_out,
           final_w, layer_idx, final_norm):
    b, seq, d = x.shape
    bs = _block_sizes(seq)
    d_mix = w_out.shape[0]
    diff_w = d_mix // 2
    gla_w = d_mix - diff_w
    qk_cols = DIFF_HEADS * LANES_V7X
    gla_qk = w_gate_up.shape[1]
    rank = w_gate_up.shape[0]
    o_dq, o_dk, o_dv = 0, qk_cols, 2 * qk_cols
    o_dg = o_dv + diff_w
    o_gq = o_dg + diff_w
    o_gk = o_gq + gla_qk
    o_gv = o_gk + gla_qk
    o_gg = o_gv + gla_w
    o_ga = o_gg + gla_w
    assert w_in.shape[1] == o_ga + rank
    wa = jnp.concatenate([w_in[:, o_dq:o_dg], w_in[:, o_gq:o_gg]], axis=1).astype(BF16)
    wg = jnp.concatenate([w_in[:, o_dg:o_gq], w_in[:, o_gg:o_ga]], axis=1).astype(BF16)
    wr = jnp.pad(w_in[:, o_ga:], ((0, 0), (0, LANES_V7X - rank))).astype(BF16)
    wgu = jnp.pad(w_gate_up, ((0, LANES_V7X - rank), (0, 0))).astype(BF16)

    d_half = LANES_V7X // 2
    x2 = x.reshape(b * seq, d)
    pa, pg, pr = _inproj(x2, norm_w.reshape(1, d), wa, wg, wr,
                         q_cols=qk_cols, q_scale=(d_half ** -0.5) * LOG2E, tm=bs["tm_in"])
    pa = pa.reshape(b, seq, -1)
    pg = pg.reshape(b, seq, -1)
    pr = pr.reshape(b, seq, -1)

    lam_init = _lambda_init(layer_idx)
    fqc, fkc = _alibi_feature_consts(DIFF_HEADS)
    row = lambda v: v.reshape(1, -1).astype(F32)
    ya = _diff_attention(pa, pg, row(lq1), row(lk1), row(lq2), row(lk2), row(subln_w), fqc, fkc,
                         lam_init=lam_init, tq=bs["tq"], tk=bs["tk"])
    yb = _gla(pa, pg, pr, wgu, row(b_gate), row(gla_norm_w), lb=bs["lb"])

    out = _outproj(x2, ya.reshape(b * seq, -1), yb.reshape(b * seq, -1), w_out.astype(BF16),
                   row(final_w), final_norm=final_norm, tm=bs["tm_out"])
    return out.reshape(b, seq, d)


def kernel(x, norm_w, w_in, w_gate_up, b_gate, lambda_q1, lambda_k1, lambda_q2, lambda_k2,
           diff_subln_w, gla_norm_w, w_out, final_norm_w):
    depth = norm_w.shape[0]
    h = x
    for layer in range(depth):
        h = _layer(h, norm_w[layer], w_in[layer], w_gate_up[layer], b_gate[layer],
                   lambda_q1[layer], lambda_k1[layer], lambda_q2[layer], lambda_k2[layer],
                   diff_subln_w[layer], gla_norm_w[layer], w_out[layer], final_norm_w,
                   layer, final_norm=(layer == depth - 1))
    return h
```

```python
import functools
import math

import numpy as np
import jax
import jax.numpy as jnp
from jax import lax
from jax.experimental import pallas as pl
from jax.experimental.pallas import tpu as pltpu

EPS = 1e-6
DIFF_HEADS = 4
GLA_HEADS = 4
GLA_GATE_NORMALIZER = 16.0
GLA_CHUNK = 64
LOG2E = math.log2(math.e)

LANES_V7X = 128
VMEM_BYTES_V7X = 64 * 1024 * 1024
VMEM_LIMIT_V7X = 56 * 1024 * 1024

F32 = jnp.float32
BF16 = jnp.bfloat16
NT_DIMS = (((1,), (1,)), ((), ()))
TN_DIMS = (((0,), (0,)), ((), ()))

N_SPLIT = 3


def _lambda_init(layer_idx):
    return 0.8 - 0.6 * math.exp(-0.3 * layer_idx)


def _alibi_feature_consts(n_heads):
    slopes = np.exp2(-8.0 * np.arange(1, n_heads + 1, dtype=np.float64) / n_heads)
    per = n_heads * N_SPLIT
    fq = np.zeros((n_heads, 1, LANES_V7X), np.float32)
    fk = np.zeros((1, LANES_V7X), np.float32)
    for h in range(n_heads):
        rem = np.float32(slopes[h] * LOG2E)
        for x in range(N_SPLIT):
            p = np.float32(rem).astype(BF16).astype(np.float32)
            rem = np.float32(rem - p)
            fk[0, h * N_SPLIT + x] = -p
            fk[0, per + h * N_SPLIT + x] = -p
            fq[h, 0, 2 * per + x] = p
            fq[h, 0, 2 * per + N_SPLIT + x] = p
    return jnp.asarray(fq), jnp.asarray(fk)


def _pos_features(pos, lane, first_hi, first_lo):
    hi = (pos >> 8) << 8
    lo = pos & 255
    in_hi = (lane >= first_hi) & (lane < first_hi + N_SPLIT)
    in_lo = (lane >= first_lo) & (lane < first_lo + N_SPLIT)
    return jnp.where(in_hi, hi, jnp.where(in_lo, lo, 0)).astype(F32)


def _inproj_kernel(x_ref, nw_ref, wa_ref, wg_ref, wr_ref, oa_ref, og_ref, or_ref, *, q_cols, q_scale):
    x = x_ref[...]
    ms = jnp.mean(x * x, axis=-1, keepdims=True)
    h = (x * lax.rsqrt(ms + EPS) * nw_ref[...]).astype(BF16)
    acc = jnp.dot(h, wa_ref[...], preferred_element_type=F32)
    oa_ref[:, :q_cols] = (acc[:, :q_cols] * q_scale).astype(BF16)
    oa_ref[:, q_cols:] = acc[:, q_cols:].astype(BF16)
    og_ref[...] = jnp.dot(h, wg_ref[...], preferred_element_type=F32)
    or_ref[...] = jnp.dot(h, wr_ref[...], preferred_element_type=F32).astype(BF16)


def _inproj(x2, nw, wa, wg, wr, *, q_cols, q_scale, tm):
    n, d = x2.shape
    na, ng, nr = wa.shape[1], wg.shape[1], wr.shape[1]
    kern = functools.partial(_inproj_kernel, q_cols=q_cols, q_scale=q_scale)
    return pl.pallas_call(
        kern,
        grid=(n // tm,),
        in_specs=[
            pl.BlockSpec((tm, d), lambda i: (i, 0)),
            pl.BlockSpec((1, d), lambda i: (0, 0)),
            pl.BlockSpec((d, na), lambda i: (0, 0)),
            pl.BlockSpec((d, ng), lambda i: (0, 0)),
            pl.BlockSpec((d, nr), lambda i: (0, 0)),
        ],
        out_specs=[
            pl.BlockSpec((tm, na), lambda i: (i, 0)),
            pl.BlockSpec((tm, ng), lambda i: (i, 0)),
            pl.BlockSpec((tm, nr), lambda i: (i, 0)),
        ],
        out_shape=[
            jax.ShapeDtypeStruct((n, na), BF16),
            jax.ShapeDtypeStruct((n, ng), F32),
            jax.ShapeDtypeStruct((n, nr), BF16),
        ],
        compiler_params=pltpu.CompilerParams(
            dimension_semantics=("arbitrary",), vmem_limit_bytes=VMEM_LIMIT_V7X),
        name="inproj",
    )(x2, nw, wa, wg, wr)


def _attn_kernel(q_ref, k_ref, v_ref, dg_ref, lq1_ref, lk1_ref, lq2_ref, lk2_ref, sw_ref,
                 fqc_ref, fkc_ref, o_ref, kf_ref, qp_ref, m_ref, acc_ref,
                 *, seq, tq, tk, heads, lam_init):
    dh = LANES_V7X
    half = dh // 2
    n_k = seq // tk
    qi = pl.program_id(1)
    qstart = pl.multiple_of(qi * tq, tq)

    @pl.when((pl.program_id(0) == 0) & (qi == 0))
    def _():
        fkc = fkc_ref[...]
        lane_k = lax.broadcasted_iota(jnp.int32, (tk, dh), 1)
        row_k = lax.broadcasted_iota(jnp.int32, (tk, dh), 0)
        per = heads * N_SPLIT

        def build_kf(j, carry):
            start = pl.multiple_of(j * tk, tk)
            f = _pos_features(row_k + start, lane_k, 2 * per, 2 * per + N_SPLIT) + fkc
            kf_ref[pl.ds(start, tk), :] = f.astype(BF16)
            return carry

        lax.fori_loop(0, n_k, build_kf, 0)

    lam = (jnp.exp(jnp.sum(lq1_ref[...] * lk1_ref[...], axis=-1, keepdims=True))
           - jnp.exp(jnp.sum(lq2_ref[...] * lk2_ref[...], axis=-1, keepdims=True))
           + lam_init)

    lane_q = lax.broadcasted_iota(jnp.int32, (tq, dh), 1)
    row_q = lax.broadcasted_iota(jnp.int32, (tq, dh), 0)
    for h in range(heads):
        q = q_ref[:, h * dh:(h + 1) * dh]
        zero = jnp.zeros_like(q)
        fq = (_pos_features(row_q + qstart, lane_q, h * N_SPLIT, (heads + h) * N_SPLIT)
              + fqc_ref[h]).astype(BF16)
        qp_ref[h, 0:tq, 0:dh] = jnp.where(lane_q < half, q, zero)
        qp_ref[h, tq:2 * tq, 0:dh] = jnp.where(lane_q >= half, q, zero)
        qp_ref[h, 0:tq, dh:2 * dh] = fq
        qp_ref[h, tq:2 * tq, dh:2 * dh] = fq
    m_ref[...] = jnp.full(m_ref.shape, -jnp.inf, F32)
    acc_ref[...] = jnp.zeros(acc_ref.shape, F32)

    ones_blk = jnp.where(lax.broadcasted_iota(jnp.int32, (tk, dh), 1) == 0, 1.0, 0.0).astype(BF16)
    r2 = lax.broadcasted_iota(jnp.int32, (2 * tq, tk), 0)
    c2 = lax.broadcasted_iota(jnp.int32, (2 * tq, tk), 1)
    col_minus_row = c2 - jnp.where(r2 >= tq, r2 - tq, r2)

    def scores(h, j):
        kstart = pl.multiple_of(j * tk, tk)
        kp = jnp.concatenate([k_ref[pl.ds(kstart, tk), h * dh:(h + 1) * dh],
                              kf_ref[pl.ds(kstart, tk), :]], axis=1)
        return lax.dot_general(qp_ref[h], kp, NT_DIMS, preferred_element_type=F32)

    def kv_step(h, j, s, masked):
        kstart = pl.multiple_of(j * tk, tk)
        if masked:
            s = jnp.where(col_minus_row <= qstart - kstart, s, -jnp.inf)
        m_old = m_ref[h]
        m_new = jnp.maximum(m_old, jnp.max(s, axis=1, keepdims=True))
        alpha = jnp.exp2(m_old - m_new)
        p = jnp.exp2(s - jnp.concatenate([m_new] * (tk // dh), axis=1)).astype(BF16)
        vp = jnp.concatenate([v_ref[pl.ds(kstart, tk), h * dh:(h + 1) * dh], ones_blk], axis=1)
        pv = jnp.dot(p, vp, preferred_element_type=F32)
        acc_ref[h] = jnp.concatenate([alpha, alpha], axis=1) * acc_ref[h] + pv
        m_ref[h] = m_new

    n_full = qstart // tk

    def all_heads(j, masked):
        s_all = [scores(h, j) for h in range(heads)]
        for h in range(heads):
            kv_step(h, j, s_all[h], masked)

    def full_step(j, c):
        all_heads(j, masked=False)
        return c

    lax.fori_loop(0, n_full, full_step, 0)
    all_heads(n_full, masked=True)

    for h in range(heads):
        acc = acc_ref[h]
        o0 = acc[0:tq, 0:dh]
        o1 = acc[tq:2 * tq, 0:dh]
        l0 = acc[0:tq, dh:dh + 1]
        l1 = acc[tq:2 * tq, dh:dh + 1]
        o = o0 * (1.0 / l0) - lam * (o1 * (1.0 / l1))
        ms = jnp.mean(o * o, axis=-1, keepdims=True)
        y = (o * lax.rsqrt(ms + EPS) * sw_ref[...]) * (1.0 - lam_init)
        g = dg_ref[:, h * dh:(h + 1) * dh]
        y = y * (g * (1.0 / (1.0 + jnp.exp(-g))))
        o_ref[:, h * dh:(h + 1) * dh] = y.astype(o_ref.dtype)


def _diff_attention(pa, pg, lq1, lk1, lq2, lk2, sw, fqc, fkc, *, lam_init, tq, tk):
    b, seq, _ = pa.shape
    dh = LANES_V7X
    h = DIFF_HEADS
    hw = h * dh
    assert tk % tq == 0 and seq % tk == 0 and tq <= 256
    kern = functools.partial(_attn_kernel, seq=seq, tq=tq, tk=tk, heads=h, lam_init=lam_init)
    vec = lambda n: pl.BlockSpec((1, n), lambda i, j: (0, 0))
    return pl.pallas_call(
        kern,
        grid=(b, seq // tq),
        in_specs=[
            pl.BlockSpec((None, tq, hw), lambda i, j: (i, j, 0)),
            pl.BlockSpec((None, seq, hw), lambda i, j: (i, 0, 1)),
            pl.BlockSpec((None, seq, hw), lambda i, j: (i, 0, 2)),
            pl.BlockSpec((None, tq, hw), lambda i, j: (i, j, 0)),
            vec(lq1.shape[1]), vec(lk1.shape[1]), vec(lq2.shape[1]), vec(lk2.shape[1]),
            vec(dh),
            pl.BlockSpec((h, 1, dh), lambda i, j: (0, 0, 0)),
            vec(dh),
        ],
        out_specs=pl.BlockSpec((None, tq, hw), lambda i, j: (i, j, 0)),
        out_shape=jax.ShapeDtypeStruct((b, seq, hw), BF16),
        scratch_shapes=[
            pltpu.VMEM((seq, dh), BF16),
            pltpu.VMEM((h, 2 * tq, 2 * dh), BF16),
            pltpu.VMEM((h, 2 * tq, dh), F32),
            pltpu.VMEM((h, 2 * tq, 2 * dh), F32),
        ],
        compiler_params=pltpu.CompilerParams(
            dimension_semantics=("arbitrary", "arbitrary"), vmem_limit_bytes=VMEM_LIMIT_V7X),
        name="diff_attn",
    )(pa, pa, pa, pg, lq1, lk1, lq2, lk2, sw, fqc, fkc)


def _gla_kernel(q_ref, k_ref, v_ref, gg_ref, ga_ref, wgu_ref, bg_ref, nw_ref, o_ref, st_ref,
                *, lb, chunk, heads):
    dk = q_ref.shape[1] // heads
    dv = v_ref.shape[1] // heads
    n_chunks = lb // chunk

    @pl.when(pl.program_id(1) == 0)
    def _():
        st_ref[...] = jnp.zeros(st_ref.shape, F32)

    z = jnp.dot(ga_ref[...], wgu_ref[...], preferred_element_type=F32) + bg_ref[...]
    g = (jnp.minimum(z, 0.0) - jnp.log(1.0 + jnp.exp(-jnp.abs(z)))) * (1.0 / GLA_GATE_NORMALIZER)

    r = lax.broadcasted_iota(jnp.int32, (lb, lb), 0)
    c = lax.broadcasted_iota(jnp.int32, (lb, lb), 1)
    same_chunk = (r // chunk) == (c // chunk)
    causal = same_chunk & (c <= r)
    ltri = jnp.where(causal, 1.0, 0.0).astype(BF16)
    lfull = jnp.where(same_chunk, 1.0, 0.0).astype(BF16)
    g_hi = g.astype(BF16)
    g_lo = (g - g_hi.astype(F32)).astype(BF16)
    ghl = jnp.concatenate([g_hi, g_lo], axis=1)
    bb = jnp.dot(ltri, ghl, preferred_element_type=F32)
    bl = jnp.dot(lfull, ghl, preferred_element_type=F32)
    nk = heads * dk
    b_cum = bb[:, :nk] + bb[:, nk:]
    b_last = bl[:, :nk] + bl[:, nk:]

    q = q_ref[...].astype(F32) * (dk ** -0.5)
    k = k_ref[...].astype(F32)
    q_in = (q * jnp.exp(b_cum)).astype(BF16)
    k_in = (k * jnp.exp(-b_cum)).astype(BF16)
    k_st = (k * jnp.exp(b_last - b_cum)).astype(BF16)
    decay = jnp.exp(b_last)

    head_of_lane = lax.broadcasted_iota(jnp.int32, (lb, nk), 1) // dk
    zero_bf = jnp.zeros((lb, nk), BF16)

    o_intra = []
    k_st_h = []
    for h in range(heads):
        sel = head_of_lane == h
        a = lax.dot_general(jnp.where(sel, q_in, zero_bf), k_in, NT_DIMS, preferred_element_type=F32)
        a = jnp.where(causal, a, 0.0).astype(BF16)
        o_intra.append(jnp.dot(a, v_ref[:, h * dv:(h + 1) * dv], preferred_element_type=F32))
        k_st_h.append(jnp.where(sel, k_st, zero_bf))

    o_inter = []
    for ci in range(n_chunks):
        rows = slice(ci * chunk, (ci + 1) * chunk)
        st = st_ref[...]
        o_inter.append(lax.dot_general(q_in[rows], st.astype(BF16), NT_DIMS, preferred_element_type=F32))
        kv_t = [lax.dot_general(v_ref[rows, h * dv:(h + 1) * dv], k_st_h[h][rows], TN_DIMS,
                                preferred_element_type=F32) for h in range(heads)]
        st_ref[...] = st * decay[ci * chunk:ci * chunk + 1, :] + jnp.concatenate(kv_t, axis=0)
    o_inter = jnp.concatenate(o_inter, axis=0)

    for h in range(heads):
        cols = slice(h * dv, (h + 1) * dv)
        o = o_intra[h] + o_inter[:, cols]
        ms = jnp.mean(o * o, axis=-1, keepdims=True)
        y = o * lax.rsqrt(ms + EPS) * nw_ref[...]
        gate = gg_ref[:, cols]
        y = y * (gate * (1.0 / (1.0 + jnp.exp(-gate))))
        o_ref[:, cols] = y.astype(o_ref.dtype)


def _gla(pa, pg, pr, wgu, bg, nw, *, lb):
    b, seq, _ = pa.shape
    heads = GLA_HEADS
    nk = wgu.shape[1]
    nv = pg.shape[2] // 2
    a_q = 3 * DIFF_HEADS * LANES_V7X
    assert a_q % nk == 0 and (a_q + 2 * nk) % nv == 0 and seq % lb == 0 and lb % GLA_CHUNK == 0
    kern = functools.partial(_gla_kernel, lb=lb, chunk=GLA_CHUNK, heads=heads)
    return pl.pallas_call(
        kern,
        grid=(b, seq // lb),
        in_specs=[
            pl.BlockSpec((None, lb, nk), lambda i, t: (i, t, a_q // nk)),
            pl.BlockSpec((None, lb, nk), lambda i, t: (i, t, a_q // nk + 1)),
            pl.BlockSpec((None, lb, nv), lambda i, t: (i, t, (a_q + 2 * nk) // nv)),
            pl.BlockSpec((None, lb, nv), lambda i, t: (i, t, 1)),
            pl.BlockSpec((None, lb, pr.shape[2]), lambda i, t: (i, t, 0)),
            pl.BlockSpec(wgu.shape, lambda i, t: (0, 0)),
            pl.BlockSpec((1, nk), lambda i, t: (0, 0)),
            pl.BlockSpec((1, nv // heads), lambda i, t: (0, 0)),
        ],
        out_specs=pl.BlockSpec((None, lb, nv), lambda i, t: (i, t, 0)),
        out_shape=jax.ShapeDtypeStruct((b, seq, nv), BF16),
        scratch_shapes=[pltpu.VMEM((nv, nk), F32)],
        compiler_params=pltpu.CompilerParams(
            dimension_semantics=("arbitrary", "arbitrary"), vmem_limit_bytes=VMEM_LIMIT_V7X),
        name="gla",
    )(pa, pa, pa, pg, pr, wgu, bg, nw)


def _outproj_kernel(x_ref, ya_ref, yb_ref, w_ref, fw_ref, o_ref, *, final_norm):
    na = ya_ref.shape[1]
    mix = jnp.dot(ya_ref[...], w_ref[:na, :], preferred_element_type=F32)
    mix = mix + jnp.dot(yb_ref[...], w_ref[na:, :], preferred_element_type=F32)
    hres = x_ref[...] + mix
    if final_norm:
        ms = jnp.mean(hres * hres, axis=-1, keepdims=True)
        hres = hres * lax.rsqrt(ms + EPS) * fw_ref[...]
    o_ref[...] = hres


def _outproj(x2, ya, yb, w, fw, *, final_norm, tm):
    n, d = x2.shape
    na, nb = ya.shape[1], yb.shape[1]
    kern = functools.partial(_outproj_kernel, final_norm=final_norm)
    return pl.pallas_call(
        kern,
        grid=(n // tm,),
        in_specs=[
            pl.BlockSpec((tm, d), lambda i: (i, 0)),
            pl.BlockSpec((tm, na), lambda i: (i, 0)),
            pl.BlockSpec((tm, nb), lambda i: (i, 0)),
            pl.BlockSpec((na + nb, d), lambda i: (0, 0)),
            pl.BlockSpec((1, d), lambda i: (0, 0)),
        ],
        out_specs=pl.BlockSpec((tm, d), lambda i: (i, 0)),
        out_shape=jax.ShapeDtypeStruct((n, d), F32),
        compiler_params=pltpu.CompilerParams(
            dimension_semantics=("arbitrary",), vmem_limit_bytes=VMEM_LIMIT_V7X),
        name="outproj",
    )(x2, ya, yb, w, fw)


def _block_sizes(seq):
    return dict(tm_in=min(512, seq), tq=min(256, seq), tk=min(512, seq),
                lb=min(512, seq), tm_out=min(1024, seq))


def _layer(x, norm_w, w_in, w_gate_up, b_gate, lq1, lk1, lq2, lk2, subln_w, gla_norm_w, w_out,
           final_w, layer_idx, final_norm):
    b, seq, d = x.shape
    bs = _block_sizes(seq)
    d_mix = w_out.shape[0]
    diff_w = d_mix // 2
    gla_w = d_mix - diff_w
    qk_cols = DIFF_HEADS * LANES_V7X
    gla_qk = w_gate_up.shape[1]
    rank = w_gate_up.shape[0]
    o_dq, o_dk, o_dv = 0, qk_cols, 2 * qk_cols
    o_dg = o_dv + diff_w
    o_gq = o_dg + diff_w
    o_gk = o_gq + gla_qk
    o_gv = o_gk + gla_qk
    o_gg = o_gv + gla_w
    o_ga = o_gg + gla_w
    assert w_in.shape[1] == o_ga + rank
    wa = jnp.concatenate([w_in[:, o_dq:o_dg], w_in[:, o_gq:o_gg]], axis=1).astype(BF16)
    wg = jnp.concatenate([w_in[:, o_dg:o_gq], w_in[:, o_gg:o_ga]], axis=1).astype(BF16)
    wr = jnp.pad(w_in[:, o_ga:], ((0, 0), (0, LANES_V7X - rank))).astype(BF16)
    wgu = jnp.pad(w_gate_up, ((0, LANES_V7X - rank), (0, 0))).astype(BF16)

    d_half = LANES_V7X // 2
    x2 = x.reshape(b * seq, d)
    pa, pg, pr = _inproj(x2, norm_w.reshape(1, d), wa, wg, wr,
                         q_cols=qk_cols, q_scale=(d_half ** -0.5) * LOG2E, tm=bs["tm_in"])
    pa = pa.reshape(b, seq, -1)
    pg = pg.reshape(b, seq, -1)
    pr = pr.reshape(b, seq, -1)

    lam_init = _lambda_init(layer_idx)
    fqc, fkc = _alibi_feature_consts(DIFF_HEADS)
    row = lambda v: v.reshape(1, -1).astype(F32)
    ya = _diff_attention(pa, pg, row(lq1), row(lk1), row(lq2), row(lk2), row(subln_w), fqc, fkc,
                         lam_init=lam_init, tq=bs["tq"], tk=bs["tk"])
    yb = _gla(pa, pg, pr, wgu, row(b_gate), row(gla_norm_w), lb=bs["lb"])

    out = _outproj(x2, ya.reshape(b * seq, -1), yb.reshape(b * seq, -1), w_out.astype(BF16),
                   row(final_w), final_norm=final_norm, tm=bs["tm_out"])
    return out.reshape(b, seq, d)


def kernel(x, norm_w, w_in, w_gate_up, b_gate, lambda_q1, lambda_k1, lambda_q2, lambda_k2,
           diff_subln_w, gla_norm_w, w_out, final_norm_w):
    depth = norm_w.shape[0]
    h = x
    for layer in range(depth):
        h = _layer(h, norm_w[layer], w_in[layer], w_gate_up[layer], b_gate[layer],
                   lambda_q1[layer], lambda_k1[layer], lambda_q2[layer], lambda_k2[layer],
                   diff_subln_w[layer], gla_norm_w[layer], w_out[layer], final_norm_w,
                   layer, final_norm=(layer == depth - 1))
    return h
```

```python
import functools
import math

import numpy as np
import jax
import jax.numpy as jnp
from jax import lax
from jax.experimental import pallas as pl
from jax.experimental.pallas import tpu as pltpu

EPS = 1e-6
DIFF_HEADS = 4
GLA_HEADS = 4
GLA_GATE_NORMALIZER = 16.0
GLA_CHUNK = 64
GLA_SUB_BLOCK = 256
LOG2E = math.log2(math.e)

LANES_V7X = 128
VMEM_BYTES_V7X = 64 * 1024 * 1024
VMEM_LIMIT_V7X = 56 * 1024 * 1024

F32 = jnp.float32
BF16 = jnp.bfloat16
NT_DIMS = (((1,), (1,)), ((), ()))
TN_DIMS = (((0,), (0,)), ((), ()))

N_SPLIT = 3


def _lambda_init(layer_idx):
    return 0.8 - 0.6 * math.exp(-0.3 * layer_idx)


def _alibi_feature_consts(n_heads):
    slopes = np.exp2(-8.0 * np.arange(1, n_heads + 1, dtype=np.float64) / n_heads)
    per = n_heads * N_SPLIT
    fq = np.zeros((n_heads, 1, LANES_V7X), np.float32)
    fk = np.zeros((1, LANES_V7X), np.float32)
    for h in range(n_heads):
        rem = np.float32(slopes[h] * LOG2E)
        for x in range(N_SPLIT):
            p = np.float32(rem).astype(BF16).astype(np.float32)
            rem = np.float32(rem - p)
            fk[0, h * N_SPLIT + x] = -p
            fk[0, per + h * N_SPLIT + x] = -p
            fq[h, 0, 2 * per + x] = p
            fq[h, 0, 2 * per + N_SPLIT + x] = p
    return jnp.asarray(fq), jnp.asarray(fk)


def _pos_features(pos, lane, first_hi, first_lo):
    hi = (pos >> 8) << 8
    lo = pos & 255
    in_hi = (lane >= first_hi) & (lane < first_hi + N_SPLIT)
    in_lo = (lane >= first_lo) & (lane < first_lo + N_SPLIT)
    return jnp.where(in_hi, hi, jnp.where(in_lo, lo, 0)).astype(F32)


def _inproj_kernel(x_ref, nw_ref, wa_ref, wg_ref, wr_ref, oa_ref, og_ref, or_ref, *, q_cols, q_scale):
    x = x_ref[...]
    ms = jnp.mean(x * x, axis=-1, keepdims=True)
    h = (x * lax.rsqrt(ms + EPS) * nw_ref[...]).astype(BF16)
    acc = jnp.dot(h, wa_ref[...], preferred_element_type=F32)
    oa_ref[:, :q_cols] = (acc[:, :q_cols] * q_scale).astype(BF16)
    oa_ref[:, q_cols:] = acc[:, q_cols:].astype(BF16)
    og_ref[...] = jnp.dot(h, wg_ref[...], preferred_element_type=F32)
    or_ref[...] = jnp.dot(h, wr_ref[...], preferred_element_type=F32).astype(BF16)


def _inproj(x2, nw, wa, wg, wr, *, q_cols, q_scale, tm):
    n, d = x2.shape
    na, ng, nr = wa.shape[1], wg.shape[1], wr.shape[1]
    kern = functools.partial(_inproj_kernel, q_cols=q_cols, q_scale=q_scale)
    return pl.pallas_call(
        kern,
        grid=(n // tm,),
        in_specs=[
            pl.BlockSpec((tm, d), lambda i: (i, 0)),
            pl.BlockSpec((1, d), lambda i: (0, 0)),
            pl.BlockSpec((d, na), lambda i: (0, 0)),
            pl.BlockSpec((d, ng), lambda i: (0, 0)),
            pl.BlockSpec((d, nr), lambda i: (0, 0)),
        ],
        out_specs=[
            pl.BlockSpec((tm, na), lambda i: (i, 0)),
            pl.BlockSpec((tm, ng), lambda i: (i, 0)),
            pl.BlockSpec((tm, nr), lambda i: (i, 0)),
        ],
        out_shape=[
            jax.ShapeDtypeStruct((n, na), BF16),
            jax.ShapeDtypeStruct((n, ng), F32),
            jax.ShapeDtypeStruct((n, nr), BF16),
        ],
        compiler_params=pltpu.CompilerParams(
            dimension_semantics=("arbitrary",), vmem_limit_bytes=VMEM_LIMIT_V7X),
        name="inproj",
    )(x2, nw, wa, wg, wr)


def _attn_kernel(q_ref, k_ref, v_ref, dg_ref, lq1_ref, lk1_ref, lq2_ref, lk2_ref, sw_ref,
                 fqc_ref, fkc_ref, o_ref, kf_ref, qp_ref, m_ref, acc_ref, s_ref,
                 *, seq, tq, tk, heads, lam_init):
    dh = LANES_V7X
    half = dh // 2
    n_k = seq // tk
    qi = pl.program_id(1)
    qstart = pl.multiple_of(qi * tq, tq)

    @pl.when((pl.program_id(0) == 0) & (qi == 0))
    def _():
        fkc = fkc_ref[...]
        lane_k = lax.broadcasted_iota(jnp.int32, (tk, dh), 1)
        row_k = lax.broadcasted_iota(jnp.int32, (tk, dh), 0)
        per = heads * N_SPLIT

        def build_kf(j, carry):
            start = pl.multiple_of(j * tk, tk)
            f = _pos_features(row_k + start, lane_k, 2 * per, 2 * per + N_SPLIT) + fkc
            kf_ref[pl.ds(start, tk), :] = f.astype(BF16)
            return carry

        lax.fori_loop(0, n_k, build_kf, 0)

    lam = (jnp.exp(jnp.sum(lq1_ref[...] * lk1_ref[...], axis=-1, keepdims=True))
           - jnp.exp(jnp.sum(lq2_ref[...] * lk2_ref[...], axis=-1, keepdims=True))
           + lam_init)

    lane_q = lax.broadcasted_iota(jnp.int32, (tq, dh), 1)
    row_q = lax.broadcasted_iota(jnp.int32, (tq, dh), 0)
    for h in range(heads):
        q = q_ref[:, h * dh:(h + 1) * dh]
        zero = jnp.zeros_like(q)
        fq = (_pos_features(row_q + qstart, lane_q, h * N_SPLIT, (heads + h) * N_SPLIT)
              + fqc_ref[h]).astype(BF16)
        qp_ref[h, 0:tq, 0:dh] = jnp.where(lane_q < half, q, zero)
        qp_ref[h, tq:2 * tq, 0:dh] = jnp.where(lane_q >= half, q, zero)
        qp_ref[h, 0:tq, dh:2 * dh] = fq
        qp_ref[h, tq:2 * tq, dh:2 * dh] = fq
    m_ref[...] = jnp.full(m_ref.shape, -jnp.inf, F32)
    acc_ref[...] = jnp.zeros(acc_ref.shape, F32)

    ones_blk = jnp.where(lax.broadcasted_iota(jnp.int32, (tk, dh), 1) == 0, 1.0, 0.0).astype(BF16)
    r2 = lax.broadcasted_iota(jnp.int32, (2 * tq, tk), 0)
    c2 = lax.broadcasted_iota(jnp.int32, (2 * tq, tk), 1)
    col_minus_row = c2 - jnp.where(r2 >= tq, r2 - tq, r2)

    def scores(h, j):
        kstart = pl.multiple_of(j * tk, tk)
        kp = jnp.concatenate([k_ref[pl.ds(kstart, tk), h * dh:(h + 1) * dh],
                              kf_ref[pl.ds(kstart, tk), :]], axis=1)
        return lax.dot_general(qp_ref[h], kp, NT_DIMS, preferred_element_type=F32)

    def kv_step(h, j, s, masked):
        kstart = pl.multiple_of(j * tk, tk)
        if masked:
            s = jnp.where(col_minus_row <= qstart - kstart, s, -jnp.inf)
        m_old = m_ref[h]
        m_new = jnp.maximum(m_old, jnp.max(s, axis=1, keepdims=True))
        alpha = jnp.exp2(m_old - m_new)
        p = jnp.exp2(s - jnp.concatenate([m_new] * (tk // dh), axis=1)).astype(BF16)
        vp = jnp.concatenate([v_ref[pl.ds(kstart, tk), h * dh:(h + 1) * dh], ones_blk], axis=1)
        pv = jnp.dot(p, vp, preferred_element_type=F32)
        acc_ref[h] = jnp.concatenate([alpha, alpha], axis=1) * acc_ref[h] + pv
        m_ref[h] = m_new

    n_full = qstart // tk

    def step(j, src, dst):
        s_all = [s_ref[src, h] for h in range(heads)]
        for h in range(heads):
            s_ref[dst, h] = scores(h, j + 1)
        for h in range(heads):
            kv_step(h, j, s_all[h], masked=False)

    for h in range(heads):
        s_ref[0, h] = scores(h, 0)

    def pair(i, c):
        step(2 * i, 0, 1)
        step(2 * i + 1, 1, 0)
        return c

    lax.fori_loop(0, n_full // 2, pair, 0)
    odd = n_full % 2

    @pl.when(odd == 1)
    def _():
        step(n_full - 1, 0, 1)

    for h in range(heads):
        kv_step(h, n_full, s_ref[odd, h], masked=True)

    for h in range(heads):
        acc = acc_ref[h]
        o0 = acc[0:tq, 0:dh]
        o1 = acc[tq:2 * tq, 0:dh]
        l0 = acc[0:tq, dh:dh + 1]
        l1 = acc[tq:2 * tq, dh:dh + 1]
        o = o0 * (1.0 / l0) - lam * (o1 * (1.0 / l1))
        ms = jnp.mean(o * o, axis=-1, keepdims=True)
        y = (o * lax.rsqrt(ms + EPS) * sw_ref[...]) * (1.0 - lam_init)
        g = dg_ref[:, h * dh:(h + 1) * dh]
        y = y * (g * (1.0 / (1.0 + jnp.exp(-g))))
        o_ref[:, h * dh:(h + 1) * dh] = y.astype(o_ref.dtype)


def _diff_attention(pa, pg, lq1, lk1, lq2, lk2, sw, fqc, fkc, *, lam_init, tq, tk):
    b, seq, _ = pa.shape
    dh = LANES_V7X
    h = DIFF_HEADS
    hw = h * dh
    assert tk % tq == 0 and seq % tk == 0 and tq <= 256
    kern = functools.partial(_attn_kernel, seq=seq, tq=tq, tk=tk, heads=h, lam_init=lam_init)
    vec = lambda n: pl.BlockSpec((1, n), lambda i, j: (0, 0))
    return pl.pallas_call(
        kern,
        grid=(b, seq // tq),
        in_specs=[
            pl.BlockSpec((None, tq, hw), lambda i, j: (i, j, 0)),
            pl.BlockSpec((None, seq, hw), lambda i, j: (i, 0, 1)),
            pl.BlockSpec((None, seq, hw), lambda i, j: (i, 0, 2)),
            pl.BlockSpec((None, tq, hw), lambda i, j: (i, j, 0)),
            vec(lq1.shape[1]), vec(lk1.shape[1]), vec(lq2.shape[1]), vec(lk2.shape[1]),
            vec(dh),
            pl.BlockSpec((h, 1, dh), lambda i, j: (0, 0, 0)),
            vec(dh),
        ],
        out_specs=pl.BlockSpec((None, tq, hw), lambda i, j: (i, j, 0)),
        out_shape=jax.ShapeDtypeStruct((b, seq, hw), BF16),
        scratch_shapes=[
            pltpu.VMEM((seq, dh), BF16),
            pltpu.VMEM((h, 2 * tq, 2 * dh), BF16),
            pltpu.VMEM((h, 2 * tq, dh), F32),
            pltpu.VMEM((h, 2 * tq, 2 * dh), F32),
            pltpu.VMEM((2, h, 2 * tq, tk), F32),
        ],
        compiler_params=pltpu.CompilerParams(
            dimension_semantics=("arbitrary", "arbitrary"), vmem_limit_bytes=VMEM_LIMIT_V7X),
        name="diff_attn",
    )(pa, pa, pa, pg, lq1, lk1, lq2, lk2, sw, fqc, fkc)


def _gla_kernel(q_ref, k_ref, v_ref, gg_ref, ga_ref, wgu_ref, bg_ref, nw_ref, o_ref, st_ref,
                *, lb, chunk, heads):
    dk = q_ref.shape[1] // heads
    dv = v_ref.shape[1] // heads
    n_chunks = lb // chunk

    @pl.when(pl.program_id(1) == 0)
    def _():
        st_ref[...] = jnp.zeros(st_ref.shape, F32)

    z = jnp.dot(ga_ref[...], wgu_ref[...], preferred_element_type=F32) + bg_ref[...]
    g = (jnp.minimum(z, 0.0) - jnp.log(1.0 + jnp.exp(-jnp.abs(z)))) * (1.0 / GLA_GATE_NORMALIZER)

    sb = min(lb, GLA_SUB_BLOCK)
    n_sub = lb // sb
    r = lax.broadcasted_iota(jnp.int32, (sb, sb), 0)
    c = lax.broadcasted_iota(jnp.int32, (sb, sb), 1)
    causal = ((r // chunk) == (c // chunk)) & (c <= r)
    ltri = jnp.where(causal, 1.0, 0.0).astype(BF16)
    g_hi = g.astype(BF16)
    g_lo = (g - g_hi.astype(F32)).astype(BF16)
    ghl = jnp.concatenate([g_hi, g_lo], axis=1)
    bb = jnp.concatenate([jnp.dot(ltri, ghl[s * sb:(s + 1) * sb], preferred_element_type=F32)
                          for s in range(n_sub)], axis=0)
    nk = heads * dk
    b_cum = bb[:, :nk] + bb[:, nk:]
    b_last = jnp.concatenate(
        [jnp.broadcast_to(b_cum[(ci + 1) * chunk - 1:(ci + 1) * chunk, :], (chunk, nk))
         for ci in range(n_chunks)], axis=0)

    q = q_ref[...].astype(F32) * (dk ** -0.5)
    k = k_ref[...].astype(F32)
    q_in = (q * jnp.exp(b_cum)).astype(BF16)
    k_in = (k * jnp.exp(-b_cum)).astype(BF16)
    k_st = (k * jnp.exp(b_last - b_cum)).astype(BF16)
    decay = jnp.exp(b_last)

    head_of_lane = lax.broadcasted_iota(jnp.int32, (lb, nk), 1) // dk
    zero_bf = jnp.zeros((lb, nk), BF16)

    o_intra = []
    k_st_h = []
    for h in range(heads):
        sel = head_of_lane == h
        q_h = jnp.where(sel, q_in, zero_bf)
        parts = []
        for s in range(n_sub):
            rows = slice(s * sb, (s + 1) * sb)
            a = lax.dot_general(q_h[rows], k_in[rows], NT_DIMS, preferred_element_type=F32)
            a = jnp.where(causal, a, 0.0).astype(BF16)
            parts.append(jnp.dot(a, v_ref[rows, h * dv:(h + 1) * dv], preferred_element_type=F32))
        o_intra.append(jnp.concatenate(parts, axis=0))
        k_st_h.append(jnp.where(sel, k_st, zero_bf))

    o_inter = []
    for ci in range(n_chunks):
        rows = slice(ci * chunk, (ci + 1) * chunk)
        st = st_ref[...]
        o_inter.append(lax.dot_general(q_in[rows], st.astype(BF16), NT_DIMS, preferred_element_type=F32))
        kv_t = [lax.dot_general(v_ref[rows, h * dv:(h + 1) * dv], k_st_h[h][rows], TN_DIMS,
                                preferred_element_type=F32) for h in range(heads)]
        st_ref[...] = st * decay[ci * chunk:ci * chunk + 1, :] + jnp.concatenate(kv_t, axis=0)
    o_inter = jnp.concatenate(o_inter, axis=0)

    for h in range(heads):
        cols = slice(h * dv, (h + 1) * dv)
        o = o_intra[h] + o_inter[:, cols]
        ms = jnp.mean(o * o, axis=-1, keepdims=True)
        y = o * lax.rsqrt(ms + EPS) * nw_ref[...]
        gate = gg_ref[:, cols]
        y = y * (gate * (1.0 / (1.0 + jnp.exp(-gate))))
        o_ref[:, cols] = y.astype(o_ref.dtype)


def _gla(pa, pg, pr, wgu, bg, nw, *, lb):
    b, seq, _ = pa.shape
    heads = GLA_HEADS
    nk = wgu.shape[1]
    nv = pg.shape[2] // 2
    a_q = 3 * DIFF_HEADS * LANES_V7X
    assert a_q % nk == 0 and (a_q + 2 * nk) % nv == 0 and seq % lb == 0 and lb % GLA_CHUNK == 0
    kern = functools.partial(_gla_kernel, lb=lb, chunk=GLA_CHUNK, heads=heads)
    return pl.pallas_call(
        kern,
        grid=(b, seq // lb),
        in_specs=[
            pl.BlockSpec((None, lb, nk), lambda i, t: (i, t, a_q // nk)),
            pl.BlockSpec((None, lb, nk), lambda i, t: (i, t, a_q // nk + 1)),
            pl.BlockSpec((None, lb, nv), lambda i, t: (i, t, (a_q + 2 * nk) // nv)),
            pl.BlockSpec((None, lb, nv), lambda i, t: (i, t, 1)),
            pl.BlockSpec((None, lb, pr.shape[2]), lambda i, t: (i, t, 0)),
            pl.BlockSpec(wgu.shape, lambda i, t: (0, 0)),
            pl.BlockSpec((1, nk), lambda i, t: (0, 0)),
            pl.BlockSpec((1, nv // heads), lambda i, t: (0, 0)),
        ],
        out_specs=pl.BlockSpec((None, lb, nv), lambda i, t: (i, t, 0)),
        out_shape=jax.ShapeDtypeStruct((b, seq, nv), BF16),
        scratch_shapes=[pltpu.VMEM((nv, nk), F32)],
        compiler_params=pltpu.CompilerParams(
            dimension_semantics=("arbitrary", "arbitrary"), vmem_limit_bytes=VMEM_LIMIT_V7X),
        name="gla",
    )(pa, pa, pa, pg, pr, wgu, bg, nw)


def _outproj_kernel(x_ref, ya_ref, yb_ref, w_ref, fw_ref, o_ref, *, final_norm):
    na = ya_ref.shape[1]
    mix = jnp.dot(ya_ref[...], w_ref[:na, :], preferred_element_type=F32)
    mix = mix + jnp.dot(yb_ref[...], w_ref[na:, :], preferred_element_type=F32)
    hres = x_ref[...] + mix
    if final_norm:
        ms = jnp.mean(hres * hres, axis=-1, keepdims=True)
        hres = hres * lax.rsqrt(ms + EPS) * fw_ref[...]
    o_ref[...] = hres


def _outproj(x2, ya, yb, w, fw, *, final_norm, tm):
    n, d = x2.shape
    na, nb = ya.shape[1], yb.shape[1]
    kern = functools.partial(_outproj_kernel, final_norm=final_norm)
    return pl.pallas_call(
        kern,
        grid=(n // tm,),
        in_specs=[
            pl.BlockSpec((tm, d), lambda i: (i, 0)),
            pl.BlockSpec((tm, na), lambda i: (i, 0)),
            pl.BlockSpec((tm, nb), lambda i: (i, 0)),
            pl.BlockSpec((na + nb, d), lambda i: (0, 0)),
            pl.BlockSpec((1, d), lambda i: (0, 0)),
        ],
        out_specs=pl.BlockSpec((tm, d), lambda i: (i, 0)),
        out_shape=jax.ShapeDtypeStruct((n, d), F32),
        compiler_params=pltpu.CompilerParams(
            dimension_semantics=("arbitrary",), vmem_limit_bytes=VMEM_LIMIT_V7X),
        name="outproj",
    )(x2, ya, yb, w, fw)


def _block_sizes(seq):
    return dict(tm_in=min(512, seq), tq=min(256, seq), tk=min(512, seq),
                lb=min(512, seq), tm_out=min(1024, seq))


def _layer(x, norm_w, w_in, w_gate_up, b_gate, lq1, lk1, lq2, lk2, subln_w, gla_norm_w, w_out,
           final_w, layer_idx, final_norm):
    b, seq, d = x.shape
    bs = _block_sizes(seq)
    d_mix = w_out.shape[0]
    diff_w = d_mix // 2
    gla_w = d_mix - diff_w
    qk_cols = DIFF_HEADS * LANES_V7X
    gla_qk = w_gate_up.shape[1]
    rank = w_gate_up.shape[0]
    o_dq, o_dk, o_dv = 0, qk_cols, 2 * qk_cols
    o_dg = o_dv + diff_w
    o_gq = o_dg + diff_w
    o_gk = o_gq + gla_qk
    o_gv = o_gk + gla_qk
    o_gg = o_gv + gla_w
    o_ga = o_gg + gla_w
    assert w_in.shape[1] == o_ga + rank
    wa = jnp.concatenate([w_in[:, o_dq:o_dg], w_in[:, o_gq:o_gg]], axis=1).astype(BF16)
    wg = jnp.concatenate([w_in[:, o_dg:o_gq], w_in[:, o_gg:o_ga]], axis=1).astype(BF16)
    wr = jnp.pad(w_in[:, o_ga:], ((0, 0), (0, LANES_V7X - rank))).astype(BF16)
    wgu = jnp.pad(w_gate_up, ((0, LANES_V7X - rank), (0, 0))).astype(BF16)

    d_half = LANES_V7X // 2
    x2 = x.reshape(b * seq, d)
    pa, pg, pr = _inproj(x2, norm_w.reshape(1, d), wa, wg, wr,
                         q_cols=qk_cols, q_scale=(d_half ** -0.5) * LOG2E, tm=bs["tm_in"])
    pa = pa.reshape(b, seq, -1)
    pg = pg.reshape(b, seq, -1)
    pr = pr.reshape(b, seq, -1)

    lam_init = _lambda_init(layer_idx)
    fqc, fkc = _alibi_feature_consts(DIFF_HEADS)
    row = lambda v: v.reshape(1, -1).astype(F32)
    ya = _diff_attention(pa, pg, row(lq1), row(lk1), row(lq2), row(lk2), row(subln_w), fqc, fkc,
                         lam_init=lam_init, tq=bs["tq"], tk=bs["tk"])
    yb = _gla(pa, pg, pr, wgu, row(b_gate), row(gla_norm_w), lb=bs["lb"])

    out = _outproj(x2, ya.reshape(b * seq, -1), yb.reshape(b * seq, -1), w_out.astype(BF16),
                   row(final_w), final_norm=final_norm, tm=bs["tm_out"])
    return out.reshape(b, seq, d)


def kernel(x, norm_w, w_in, w_gate_up, b_gate, lambda_q1, lambda_k1, lambda_q2, lambda_k2,
           diff_subln_w, gla_norm_w, w_out, final_norm_w):
    depth = norm_w.shape[0]
    h = x
    for layer in range(depth):
        h = _layer(h, norm_w[layer], w_in[layer], w_gate_up[layer], b_gate[layer],
                   lambda_q1[layer], lambda_k1[layer], lambda_q2[layer], lambda_k2[layer],
                   diff_subln_w[layer], gla_norm_w[layer], w_out[layer], final_norm_w,
                   layer, final_norm=(layer == depth - 1))
    return h
```

```python
import functools
import math

import numpy as np
import jax
import jax.numpy as jnp
from jax import lax
from jax.experimental import pallas as pl
from jax.experimental.pallas import tpu as pltpu

EPS = 1e-6
DIFF_HEADS = 4
GLA_HEADS = 4
GLA_GATE_NORMALIZER = 16.0
GLA_CHUNK = 64
GLA_SUB_BLOCK = 256
LOG2E = math.log2(math.e)

LANES_V7X = 128
VMEM_BYTES_V7X = 64 * 1024 * 1024
VMEM_LIMIT_V7X = 56 * 1024 * 1024

F32 = jnp.float32
BF16 = jnp.bfloat16
NT_DIMS = (((1,), (1,)), ((), ()))
TN_DIMS = (((0,), (0,)), ((), ()))

N_SPLIT = 3
ONES_ROWS = 16


def _lambda_init(layer_idx):
    return 0.8 - 0.6 * math.exp(-0.3 * layer_idx)


def _alibi_feature_consts(n_heads):
    slopes = np.exp2(-8.0 * np.arange(1, n_heads + 1, dtype=np.float64) / n_heads)
    per = n_heads * N_SPLIT
    fq = np.zeros((n_heads, 1, LANES_V7X), np.float32)
    fk = np.zeros((1, LANES_V7X), np.float32)
    for h in range(n_heads):
        rem = np.float32(slopes[h] * LOG2E)
        for x in range(N_SPLIT):
            p = np.float32(rem).astype(BF16).astype(np.float32)
            rem = np.float32(rem - p)
            fk[0, h * N_SPLIT + x] = -p
            fk[0, per + h * N_SPLIT + x] = -p
            fq[h, 0, 2 * per + x] = p
            fq[h, 0, 2 * per + N_SPLIT + x] = p
    return jnp.asarray(fq), jnp.asarray(fk)


def _pos_features(pos, lane, first_hi, first_lo):
    hi = (pos >> 8) << 8
    lo = pos & 255
    in_hi = (lane >= first_hi) & (lane < first_hi + N_SPLIT)
    in_lo = (lane >= first_lo) & (lane < first_lo + N_SPLIT)
    return jnp.where(in_hi, hi, jnp.where(in_lo, lo, 0)).astype(F32)


def _inproj_kernel(x_ref, nw_ref, wa_ref, wg_ref, wr_ref, oa_ref, og_ref, or_ref, ovt_ref,
                   *, q_cols, q_scale):
    x = x_ref[...]
    ms = jnp.mean(x * x, axis=-1, keepdims=True)
    h = (x * lax.rsqrt(ms + EPS) * nw_ref[...]).astype(BF16)
    acc = jnp.dot(h, wa_ref[...], preferred_element_type=F32)
    oa_ref[:, :q_cols] = (acc[:, :q_cols] * q_scale).astype(BF16)
    oa_ref[:, q_cols:] = acc[:, q_cols:].astype(BF16)
    ovt_ref[...] = acc[:, 2 * q_cols:3 * q_cols].T.astype(BF16)
    og_ref[...] = jnp.dot(h, wg_ref[...], preferred_element_type=F32)
    or_ref[...] = jnp.dot(h, wr_ref[...], preferred_element_type=F32).astype(BF16)


def _inproj(x2, nw, wa, wg, wr, *, q_cols, q_scale, tm):
    n, d = x2.shape
    na, ng, nr = wa.shape[1], wg.shape[1], wr.shape[1]
    kern = functools.partial(_inproj_kernel, q_cols=q_cols, q_scale=q_scale)
    return pl.pallas_call(
        kern,
        grid=(n // tm,),
        in_specs=[
            pl.BlockSpec((tm, d), lambda i: (i, 0)),
            pl.BlockSpec((1, d), lambda i: (0, 0)),
            pl.BlockSpec((d, na), lambda i: (0, 0)),
            pl.BlockSpec((d, ng), lambda i: (0, 0)),
            pl.BlockSpec((d, nr), lambda i: (0, 0)),
        ],
        out_specs=[
            pl.BlockSpec((tm, na), lambda i: (i, 0)),
            pl.BlockSpec((tm, ng), lambda i: (i, 0)),
            pl.BlockSpec((tm, nr), lambda i: (i, 0)),
            pl.BlockSpec((None, q_cols, tm), lambda i: (i, 0, 0)),
        ],
        out_shape=[
            jax.ShapeDtypeStruct((n, na), BF16),
            jax.ShapeDtypeStruct((n, ng), F32),
            jax.ShapeDtypeStruct((n, nr), BF16),
            jax.ShapeDtypeStruct((n // tm, q_cols, tm), BF16),
        ],
        compiler_params=pltpu.CompilerParams(
            dimension_semantics=("arbitrary",), vmem_limit_bytes=VMEM_LIMIT_V7X),
        name="inproj",
    )(x2, nw, wa, wg, wr)


def _attn_kernel(q_ref, k_ref, vt_ref, dg_ref, lq1_ref, lk1_ref, lq2_ref, lk2_ref, sw_ref,
                 fqc_ref, fkc_ref, o_ref, kf_ref, qp_ref, m_ref, acc_ref, s_ref,
                 *, seq, tq, tk, heads, lam_init):
    dh = LANES_V7X
    half = dh // 2
    n_k = seq // tk
    qi = pl.program_id(1)
    qstart = pl.multiple_of(qi * tq, tq)

    @pl.when((pl.program_id(0) == 0) & (qi == 0))
    def _():
        fkc = fkc_ref[...]
        lane_k = lax.broadcasted_iota(jnp.int32, (tk, dh), 1)
        row_k = lax.broadcasted_iota(jnp.int32, (tk, dh), 0)
        per = heads * N_SPLIT

        def build_kf(j, carry):
            start = pl.multiple_of(j * tk, tk)
            f = _pos_features(row_k + start, lane_k, 2 * per, 2 * per + N_SPLIT) + fkc
            kf_ref[pl.ds(start, tk), :] = f.astype(BF16)
            return carry

        lax.fori_loop(0, n_k, build_kf, 0)

    lam = (jnp.exp(jnp.sum(lq1_ref[...] * lk1_ref[...], axis=-1, keepdims=True))
           - jnp.exp(jnp.sum(lq2_ref[...] * lk2_ref[...], axis=-1, keepdims=True))
           + lam_init)

    lane_q = lax.broadcasted_iota(jnp.int32, (tq, dh), 1)
    row_q = lax.broadcasted_iota(jnp.int32, (tq, dh), 0)
    for h in range(heads):
        q = q_ref[:, h * dh:(h + 1) * dh]
        zero = jnp.zeros_like(q)
        fq = (_pos_features(row_q + qstart, lane_q, h * N_SPLIT, (heads + h) * N_SPLIT)
              + fqc_ref[h]).astype(BF16)
        qp_ref[h, 0:tq, 0:dh] = jnp.where(lane_q < half, q, zero)
        qp_ref[h, tq:2 * tq, 0:dh] = jnp.where(lane_q >= half, q, zero)
        qp_ref[h, 0:tq, dh:2 * dh] = fq
        qp_ref[h, tq:2 * tq, dh:2 * dh] = fq
    m_ref[...] = jnp.full(m_ref.shape, -jnp.inf, F32)
    acc_ref[...] = jnp.zeros(acc_ref.shape, F32)

    ones_rows = jnp.where(lax.broadcasted_iota(jnp.int32, (ONES_ROWS, tk), 0) == 0, 1.0, 0.0).astype(BF16)
    r2 = lax.broadcasted_iota(jnp.int32, (tk, 2 * tq), 0)
    c2 = lax.broadcasted_iota(jnp.int32, (tk, 2 * tq), 1)
    kv_minus_q = r2 - jnp.where(c2 >= tq, c2 - tq, c2)

    def scores(h, j):
        kstart = pl.multiple_of(j * tk, tk)
        kp = jnp.concatenate([k_ref[pl.ds(kstart, tk), h * dh:(h + 1) * dh],
                              kf_ref[pl.ds(kstart, tk), :]], axis=1)
        return lax.dot_general(kp, qp_ref[h], NT_DIMS, preferred_element_type=F32)

    def kv_step(h, j, s, masked):
        kstart = pl.multiple_of(j * tk, tk)
        if masked:
            s = jnp.where(kv_minus_q <= qstart - kstart, s, -jnp.inf)
        m_old = m_ref[h]
        m_new = jnp.maximum(m_old, jnp.max(s, axis=0, keepdims=True))
        alpha = jnp.exp2(m_old - m_new)
        p = jnp.exp2(s - m_new).astype(BF16)
        vt = jnp.concatenate([vt_ref[j, h * dh:(h + 1) * dh, :], ones_rows], axis=0)
        pv = jnp.dot(vt, p, preferred_element_type=F32)
        acc_ref[h] = alpha * acc_ref[h] + pv
        m_ref[h] = m_new

    n_full = qstart // tk

    def step(j, src, dst):
        s_all = [s_ref[src, h] for h in range(heads)]
        for h in range(heads):
            s_ref[dst, h] = scores(h, j + 1)
        for h in range(heads):
            kv_step(h, j, s_all[h], masked=False)

    for h in range(heads):
        s_ref[0, h] = scores(h, 0)

    def pair(i, c):
        step(2 * i, 0, 1)
        step(2 * i + 1, 1, 0)
        return c

    lax.fori_loop(0, n_full // 2, pair, 0)
    odd = n_full % 2

    @pl.when(odd == 1)
    def _():
        step(n_full - 1, 0, 1)

    for h in range(heads):
        kv_step(h, n_full, s_ref[odd, h], masked=True)

    for h in range(heads):
        acc = acc_ref[h]
        o0 = acc[0:dh, 0:tq]
        o1 = acc[0:dh, tq:2 * tq]
        l0 = acc[dh:dh + 1, 0:tq]
        l1 = acc[dh:dh + 1, tq:2 * tq]
        o = (o0 * (1.0 / l0) - lam * (o1 * (1.0 / l1))).T
        ms = jnp.mean(o * o, axis=-1, keepdims=True)
        y = (o * lax.rsqrt(ms + EPS) * sw_ref[...]) * (1.0 - lam_init)
        g = dg_ref[:, h * dh:(h + 1) * dh]
        y = y * (g * (1.0 / (1.0 + jnp.exp(-g))))
        o_ref[:, h * dh:(h + 1) * dh] = y.astype(o_ref.dtype)


def _diff_attention(pa, pvt, pg, lq1, lk1, lq2, lk2, sw, fqc, fkc, *, lam_init, tq, tk):
    b, seq, _ = pa.shape
    dh = LANES_V7X
    h = DIFF_HEADS
    hw = h * dh
    n_k = seq // tk
    assert tk % tq == 0 and seq % tk == 0 and tq <= 256
    assert pvt.shape == (b * n_k, hw, tk)
    pvt = pvt.reshape(b, n_k, hw, tk)
    kern = functools.partial(_attn_kernel, seq=seq, tq=tq, tk=tk, heads=h, lam_init=lam_init)
    vec = lambda n: pl.BlockSpec((1, n), lambda i, j: (0, 0))
    return pl.pallas_call(
        kern,
        grid=(b, seq // tq),
        in_specs=[
            pl.BlockSpec((None, tq, hw), lambda i, j: (i, j, 0)),
            pl.BlockSpec((None, seq, hw), lambda i, j: (i, 0, 1)),
            pl.BlockSpec((None, n_k, hw, tk), lambda i, j: (i, 0, 0, 0)),
            pl.BlockSpec((None, tq, hw), lambda i, j: (i, j, 0)),
            vec(lq1.shape[1]), vec(lk1.shape[1]), vec(lq2.shape[1]), vec(lk2.shape[1]),
            vec(dh),
            pl.BlockSpec((h, 1, dh), lambda i, j: (0, 0, 0)),
            vec(dh),
        ],
        out_specs=pl.BlockSpec((None, tq, hw), lambda i, j: (i, j, 0)),
        out_shape=jax.ShapeDtypeStruct((b, seq, hw), BF16),
        scratch_shapes=[
            pltpu.VMEM((seq, dh), BF16),
            pltpu.VMEM((h, 2 * tq, 2 * dh), BF16),
            pltpu.VMEM((h, 1, 2 * tq), F32),
            pltpu.VMEM((h, dh + ONES_ROWS, 2 * tq), F32),
            pltpu.VMEM((2, h, tk, 2 * tq), F32),
        ],
        compiler_params=pltpu.CompilerParams(
            dimension_semantics=("arbitrary", "arbitrary"), vmem_limit_bytes=VMEM_LIMIT_V7X),
        name="diff_attn",
    )(pa, pa, pvt, pg, lq1, lk1, lq2, lk2, sw, fqc, fkc)


def _gla_kernel(q_ref, k_ref, v_ref, gg_ref, ga_ref, wgu_ref, bg_ref, nw_ref, o_ref, st_ref,
                *, lb, chunk, heads):
    dk = q_ref.shape[1] // heads
    dv = v_ref.shape[1] // heads
    n_chunks = lb // chunk

    @pl.when(pl.program_id(1) == 0)
    def _():
        st_ref[...] = jnp.zeros(st_ref.shape, F32)

    z = jnp.dot(ga_ref[...], wgu_ref[...], preferred_element_type=F32) + bg_ref[...]
    g = (jnp.minimum(z, 0.0) - jnp.log(1.0 + jnp.exp(-jnp.abs(z)))) * (1.0 / GLA_GATE_NORMALIZER)

    sb = min(lb, GLA_SUB_BLOCK)
    n_sub = lb // sb
    r = lax.broadcasted_iota(jnp.int32, (sb, sb), 0)
    c = lax.broadcasted_iota(jnp.int32, (sb, sb), 1)
    causal = ((r // chunk) == (c // chunk)) & (c <= r)
    ltri = jnp.where(causal, 1.0, 0.0).astype(BF16)
    g_hi = g.astype(BF16)
    g_lo = (g - g_hi.astype(F32)).astype(BF16)
    ghl = jnp.concatenate([g_hi, g_lo], axis=1)
    bb = jnp.concatenate([jnp.dot(ltri, ghl[s * sb:(s + 1) * sb], preferred_element_type=F32)
                          for s in range(n_sub)], axis=0)
    nk = heads * dk
    b_cum = bb[:, :nk] + bb[:, nk:]
    b_last = jnp.concatenate(
        [jnp.broadcast_to(b_cum[(ci + 1) * chunk - 1:(ci + 1) * chunk, :], (chunk, nk))
         for ci in range(n_chunks)], axis=0)

    q = q_ref[...].astype(F32) * (dk ** -0.5)
    k = k_ref[...].astype(F32)
    q_in = (q * jnp.exp(b_cum)).astype(BF16)
    k_in = (k * jnp.exp(-b_cum)).astype(BF16)
    k_st = (k * jnp.exp(b_last - b_cum)).astype(BF16)
    decay = jnp.exp(b_last)

    head_of_lane = lax.broadcasted_iota(jnp.int32, (lb, nk), 1) // dk
    zero_bf = jnp.zeros((lb, nk), BF16)

    o_intra = []
    k_st_h = []
    for h in range(heads):
        sel = head_of_lane == h
        q_h = jnp.where(sel, q_in, zero_bf)
        parts = []
        for s in range(n_sub):
            rows = slice(s * sb, (s + 1) * sb)
            a = lax.dot_general(q_h[rows], k_in[rows], NT_DIMS, preferred_element_type=F32)
            a = jnp.where(causal, a, 0.0).astype(BF16)
            parts.append(jnp.dot(a, v_ref[rows, h * dv:(h + 1) * dv], preferred_element_type=F32))
        o_intra.append(jnp.concatenate(parts, axis=0))
        k_st_h.append(jnp.where(sel, k_st, zero_bf))

    o_inter = []
    for ci in range(n_chunks):
        rows = slice(ci * chunk, (ci + 1) * chunk)
        st = st_ref[...]
        o_inter.append(lax.dot_general(q_in[rows], st.astype(BF16), NT_DIMS, preferred_element_type=F32))
        kv_t = [lax.dot_general(v_ref[rows, h * dv:(h + 1) * dv], k_st_h[h][rows], TN_DIMS,
                                preferred_element_type=F32) for h in range(heads)]
        st_ref[...] = st * decay[ci * chunk:ci * chunk + 1, :] + jnp.concatenate(kv_t, axis=0)
    o_inter = jnp.concatenate(o_inter, axis=0)

    for h in range(heads):
        cols = slice(h * dv, (h + 1) * dv)
        o = o_intra[h] + o_inter[:, cols]
        ms = jnp.mean(o * o, axis=-1, keepdims=True)
        y = o * lax.rsqrt(ms + EPS) * nw_ref[...]
        gate = gg_ref[:, cols]
        y = y * (gate * (1.0 / (1.0 + jnp.exp(-gate))))
        o_ref[:, cols] = y.astype(o_ref.dtype)


def _gla(pa, pg, pr, wgu, bg, nw, *, lb):
    b, seq, _ = pa.shape
    heads = GLA_HEADS
    nk = wgu.shape[1]
    nv = pg.shape[2] // 2
    a_q = 3 * DIFF_HEADS * LANES_V7X
    assert a_q % nk == 0 and (a_q + 2 * nk) % nv == 0 and seq % lb == 0 and lb % GLA_CHUNK == 0
    kern = functools.partial(_gla_kernel, lb=lb, chunk=GLA_CHUNK, heads=heads)
    return pl.pallas_call(
        kern,
        grid=(b, seq // lb),
        in_specs=[
            pl.BlockSpec((None, lb, nk), lambda i, t: (i, t, a_q // nk)),
            pl.BlockSpec((None, lb, nk), lambda i, t: (i, t, a_q // nk + 1)),
            pl.BlockSpec((None, lb, nv), lambda i, t: (i, t, (a_q + 2 * nk) // nv)),
            pl.BlockSpec((None, lb, nv), lambda i, t: (i, t, 1)),
            pl.BlockSpec((None, lb, pr.shape[2]), lambda i, t: (i, t, 0)),
            pl.BlockSpec(wgu.shape, lambda i, t: (0, 0)),
            pl.BlockSpec((1, nk), lambda i, t: (0, 0)),
            pl.BlockSpec((1, nv // heads), lambda i, t: (0, 0)),
        ],
        out_specs=pl.BlockSpec((None, lb, nv), lambda i, t: (i, t, 0)),
        out_shape=jax.ShapeDtypeStruct((b, seq, nv), BF16),
        scratch_shapes=[pltpu.VMEM((nv, nk), F32)],
        compiler_params=pltpu.CompilerParams(
            dimension_semantics=("arbitrary", "arbitrary"), vmem_limit_bytes=VMEM_LIMIT_V7X),
        name="gla",
    )(pa, pa, pa, pg, pr, wgu, bg, nw)


def _outproj_kernel(x_ref, ya_ref, yb_ref, w_ref, fw_ref, o_ref, *, final_norm):
    na = ya_ref.shape[1]
    mix = jnp.dot(ya_ref[...], w_ref[:na, :], preferred_element_type=F32)
    mix = mix + jnp.dot(yb_ref[...], w_ref[na:, :], preferred_element_type=F32)
    hres = x_ref[...] + mix
    if final_norm:
        ms = jnp.mean(hres * hres, axis=-1, keepdims=True)
        hres = hres * lax.rsqrt(ms + EPS) * fw_ref[...]
    o_ref[...] = hres


def _outproj(x2, ya, yb, w, fw, *, final_norm, tm):
    n, d = x2.shape
    na, nb = ya.shape[1], yb.shape[1]
    kern = functools.partial(_outproj_kernel, final_norm=final_norm)
    return pl.pallas_call(
        kern,
        grid=(n // tm,),
        in_specs=[
            pl.BlockSpec((tm, d), lambda i: (i, 0)),
            pl.BlockSpec((tm, na), lambda i: (i, 0)),
            pl.BlockSpec((tm, nb), lambda i: (i, 0)),
            pl.BlockSpec((na + nb, d), lambda i: (0, 0)),
            pl.BlockSpec((1, d), lambda i: (0, 0)),
        ],
        out_specs=pl.BlockSpec((tm, d), lambda i: (i, 0)),
        out_shape=jax.ShapeDtypeStruct((n, d), F32),
        compiler_params=pltpu.CompilerParams(
            dimension_semantics=("arbitrary",), vmem_limit_bytes=VMEM_LIMIT_V7X),
        name="outproj",
    )(x2, ya, yb, w, fw)


def _block_sizes(seq):
    return dict(tm_in=min(512, seq), tq=min(256, seq), tk=min(512, seq),
                lb=min(512, seq), tm_out=min(1024, seq))


def _layer(x, norm_w, w_in, w_gate_up, b_gate, lq1, lk1, lq2, lk2, subln_w, gla_norm_w, w_out,
           final_w, layer_idx, final_norm):
    b, seq, d = x.shape
    bs = _block_sizes(seq)
    d_mix = w_out.shape[0]
    diff_w = d_mix // 2
    gla_w = d_mix - diff_w
    qk_cols = DIFF_HEADS * LANES_V7X
    gla_qk = w_gate_up.shape[1]
    rank = w_gate_up.shape[0]
    o_dq, o_dk, o_dv = 0, qk_cols, 2 * qk_cols
    o_dg = o_dv + diff_w
    o_gq = o_dg + diff_w
    o_gk = o_gq + gla_qk
    o_gv = o_gk + gla_qk
    o_gg = o_gv + gla_w
    o_ga = o_gg + gla_w
    assert w_in.shape[1] == o_ga + rank
    wa = jnp.concatenate([w_in[:, o_dq:o_dg], w_in[:, o_gq:o_gg]], axis=1).astype(BF16)
    wg = jnp.concatenate([w_in[:, o_dg:o_gq], w_in[:, o_gg:o_ga]], axis=1).astype(BF16)
    wr = jnp.pad(w_in[:, o_ga:], ((0, 0), (0, LANES_V7X - rank))).astype(BF16)
    wgu = jnp.pad(w_gate_up, ((0, LANES_V7X - rank), (0, 0))).astype(BF16)

    d_half = LANES_V7X // 2
    x2 = x.reshape(b * seq, d)
    assert bs["tm_in"] == bs["tk"] and diff_w == qk_cols
    pa, pg, pr, pvt = _inproj(x2, norm_w.reshape(1, d), wa, wg, wr,
                              q_cols=qk_cols, q_scale=(d_half ** -0.5) * LOG2E, tm=bs["tm_in"])
    pa = pa.reshape(b, seq, -1)
    pg = pg.reshape(b, seq, -1)
    pr = pr.reshape(b, seq, -1)

    lam_init = _lambda_init(layer_idx)
    fqc, fkc = _alibi_feature_consts(DIFF_HEADS)
    row = lambda v: v.reshape(1, -1).astype(F32)
    ya = _diff_attention(pa, pvt, pg, row(lq1), row(lk1), row(lq2), row(lk2), row(subln_w), fqc, fkc,
                         lam_init=lam_init, tq=bs["tq"], tk=bs["tk"])
    yb = _gla(pa, pg, pr, wgu, row(b_gate), row(gla_norm_w), lb=bs["lb"])

    out = _outproj(x2, ya.reshape(b * seq, -1), yb.reshape(b * seq, -1), w_out.astype(BF16),
                   row(final_w), final_norm=final_norm, tm=bs["tm_out"])
    return out.reshape(b, seq, d)


def kernel(x, norm_w, w_in, w_gate_up, b_gate, lambda_q1, lambda_k1, lambda_q2, lambda_k2,
           diff_subln_w, gla_norm_w, w_out, final_norm_w):
    depth = norm_w.shape[0]
    h = x
    for layer in range(depth):
        h = _layer(h, norm_w[layer], w_in[layer], w_gate_up[layer], b_gate[layer],
                   lambda_q1[layer], lambda_k1[layer], lambda_q2[layer], lambda_k2[layer],
                   diff_subln_w[layer], gla_norm_w[layer], w_out[layer], final_norm_w,
                   layer, final_norm=(layer == depth - 1))
    return h
```

```python
import functools
import math

import numpy as np
import jax
import jax.numpy as jnp
from jax import lax
from jax.experimental import pallas as pl
from jax.experimental.pallas import tpu as pltpu

EPS = 1e-6
DIFF_HEADS = 4
GLA_HEADS = 4
GLA_GATE_NORMALIZER = 16.0
GLA_CHUNK = 64
GLA_SUB_BLOCK = 256
LOG2E = math.log2(math.e)

LANES_V7X = 128
VMEM_BYTES_V7X = 64 * 1024 * 1024
VMEM_LIMIT_V7X = 56 * 1024 * 1024

F32 = jnp.float32
BF16 = jnp.bfloat16
NT_DIMS = (((1,), (1,)), ((), ()))
TN_DIMS = (((0,), (0,)), ((), ()))

N_SPLIT = 3
ONES_ROWS = 16


def _lambda_init(layer_idx):
    return 0.8 - 0.6 * math.exp(-0.3 * layer_idx)


def _alibi_feature_consts(n_heads):
    slopes = np.exp2(-8.0 * np.arange(1, n_heads + 1, dtype=np.float64) / n_heads)
    per = n_heads * N_SPLIT
    fq = np.zeros((n_heads, 1, LANES_V7X), np.float32)
    fk = np.zeros((1, LANES_V7X), np.float32)
    for h in range(n_heads):
        rem = np.float32(slopes[h] * LOG2E)
        for x in range(N_SPLIT):
            p = np.float32(rem).astype(BF16).astype(np.float32)
            rem = np.float32(rem - p)
            fk[0, h * N_SPLIT + x] = -p
            fk[0, per + h * N_SPLIT + x] = -p
            fq[h, 0, 2 * per + x] = p
            fq[h, 0, 2 * per + N_SPLIT + x] = p
    return jnp.asarray(fq), jnp.asarray(fk)


def _pos_features(pos, lane, first_hi, first_lo):
    hi = (pos >> 8) << 8
    lo = pos & 255
    in_hi = (lane >= first_hi) & (lane < first_hi + N_SPLIT)
    in_lo = (lane >= first_lo) & (lane < first_lo + N_SPLIT)
    return jnp.where(in_hi, hi, jnp.where(in_lo, lo, 0)).astype(F32)


def _inproj_kernel(x_ref, nw_ref, wa_ref, wg_ref, wr_ref, oa_ref, og_ref, or_ref, ovt_ref,
                   *, q_cols, q_scale):
    x = x_ref[...]
    ms = jnp.mean(x * x, axis=-1, keepdims=True)
    h = (x * lax.rsqrt(ms + EPS) * nw_ref[...]).astype(BF16)
    acc = jnp.dot(h, wa_ref[...], preferred_element_type=F32)
    oa_ref[:, :q_cols] = (acc[:, :q_cols] * q_scale).astype(BF16)
    oa_ref[:, q_cols:] = acc[:, q_cols:].astype(BF16)
    ovt_ref[...] = acc[:, 2 * q_cols:3 * q_cols].T.astype(BF16)
    og_ref[...] = jnp.dot(h, wg_ref[...], preferred_element_type=F32)
    or_ref[...] = jnp.dot(h, wr_ref[...], preferred_element_type=F32).astype(BF16)


def _inproj(x2, nw, wa, wg, wr, *, q_cols, q_scale, tm):
    n, d = x2.shape
    na, ng, nr = wa.shape[1], wg.shape[1], wr.shape[1]
    kern = functools.partial(_inproj_kernel, q_cols=q_cols, q_scale=q_scale)
    return pl.pallas_call(
        kern,
        grid=(n // tm,),
        in_specs=[
            pl.BlockSpec((tm, d), lambda i: (i, 0)),
            pl.BlockSpec((1, d), lambda i: (0, 0)),
            pl.BlockSpec((d, na), lambda i: (0, 0)),
            pl.BlockSpec((d, ng), lambda i: (0, 0)),
            pl.BlockSpec((d, nr), lambda i: (0, 0)),
        ],
        out_specs=[
            pl.BlockSpec((tm, na), lambda i: (i, 0)),
            pl.BlockSpec((tm, ng), lambda i: (i, 0)),
            pl.BlockSpec((tm, nr), lambda i: (i, 0)),
            pl.BlockSpec((None, q_cols, tm), lambda i: (i, 0, 0)),
        ],
        out_shape=[
            jax.ShapeDtypeStruct((n, na), BF16),
            jax.ShapeDtypeStruct((n, ng), F32),
            jax.ShapeDtypeStruct((n, nr), BF16),
            jax.ShapeDtypeStruct((n // tm, q_cols, tm), BF16),
        ],
        compiler_params=pltpu.CompilerParams(
            dimension_semantics=("arbitrary",), vmem_limit_bytes=VMEM_LIMIT_V7X),
        name="inproj",
    )(x2, nw, wa, wg, wr)


def _attn_kernel(q_ref, k_ref, vt_ref, dg_ref, lq1_ref, lk1_ref, lq2_ref, lk2_ref, sw_ref,
                 fqc_ref, fkc_ref, o_ref, kf_ref, qp_ref, m_ref, acc_ref, s_ref,
                 *, seq, tq, tk, heads, lam_init):
    dh = LANES_V7X
    half = dh // 2
    n_k = seq // tk
    qi = pl.program_id(1)
    qstart = pl.multiple_of(qi * tq, tq)

    @pl.when((pl.program_id(0) == 0) & (qi == 0))
    def _():
        fkc = fkc_ref[...]
        lane_k = lax.broadcasted_iota(jnp.int32, (tk, dh), 1)
        row_k = lax.broadcasted_iota(jnp.int32, (tk, dh), 0)
        per = heads * N_SPLIT

        def build_kf(j, carry):
            start = pl.multiple_of(j * tk, tk)
            f = _pos_features(row_k + start, lane_k, 2 * per, 2 * per + N_SPLIT) + fkc
            kf_ref[pl.ds(start, tk), :] = f.astype(BF16)
            return carry

        lax.fori_loop(0, n_k, build_kf, 0)

    lam = (jnp.exp(jnp.sum(lq1_ref[...] * lk1_ref[...], axis=-1, keepdims=True))
           - jnp.exp(jnp.sum(lq2_ref[...] * lk2_ref[...], axis=-1, keepdims=True))
           + lam_init)

    lane_q = lax.broadcasted_iota(jnp.int32, (tq, dh), 1)
    row_q = lax.broadcasted_iota(jnp.int32, (tq, dh), 0)
    for h in range(heads):
        q = q_ref[:, h * dh:(h + 1) * dh]
        zero = jnp.zeros_like(q)
        fq = (_pos_features(row_q + qstart, lane_q, h * N_SPLIT, (heads + h) * N_SPLIT)
              + fqc_ref[h]).astype(BF16)
        qp_ref[h, 0:tq, 0:dh] = jnp.where(lane_q < half, q, zero)
        qp_ref[h, tq:2 * tq, 0:dh] = jnp.where(lane_q >= half, q, zero)
        qp_ref[h, 0:tq, dh:2 * dh] = fq
        qp_ref[h, tq:2 * tq, dh:2 * dh] = fq
    m_ref[...] = jnp.full(m_ref.shape, -jnp.inf, F32)
    acc_ref[...] = jnp.zeros(acc_ref.shape, F32)

    ones_rows = jnp.where(lax.broadcasted_iota(jnp.int32, (ONES_ROWS, tk), 0) == 0, 1.0, 0.0).astype(BF16)
    r2 = lax.broadcasted_iota(jnp.int32, (tk, 2 * tq), 0)
    c2 = lax.broadcasted_iota(jnp.int32, (tk, 2 * tq), 1)
    kv_minus_q = r2 - jnp.where(c2 >= tq, c2 - tq, c2)
    diag_off = qstart - (qstart // tk) * tk
    mask_bias = jnp.where(kv_minus_q <= diag_off, 0.0, -jnp.inf)

    def scores(h, j):
        kstart = pl.multiple_of(j * tk, tk)
        kp = jnp.concatenate([k_ref[pl.ds(kstart, tk), h * dh:(h + 1) * dh],
                              kf_ref[pl.ds(kstart, tk), :]], axis=1)
        return lax.dot_general(kp, qp_ref[h], NT_DIMS, preferred_element_type=F32)

    def kv_step(h, j, s, masked):
        kstart = pl.multiple_of(j * tk, tk)
        if masked:
            s = s + mask_bias
        m_old = m_ref[h]
        m_new = jnp.maximum(m_old, jnp.max(s, axis=0, keepdims=True))
        alpha = jnp.exp2(m_old - m_new)
        p = jnp.exp2(s - m_new).astype(BF16)
        vt = jnp.concatenate([vt_ref[j, h * dh:(h + 1) * dh, :], ones_rows], axis=0)
        pv = jnp.dot(vt, p, preferred_element_type=F32)
        acc_ref[h] = alpha * acc_ref[h] + pv
        m_ref[h] = m_new

    n_full = qstart // tk

    def step(j, src, dst):
        s_all = [s_ref[src, h] for h in range(heads)]
        for h in range(heads):
            s_ref[dst, h] = scores(h, j + 1)
        for h in range(heads):
            kv_step(h, j, s_all[h], masked=False)

    for h in range(heads):
        s_ref[0, h] = scores(h, 0)

    def quad(i, c):
        for u in range(4):
            step(4 * i + u, u % 2, 1 - u % 2)
        return c

    n_quads = n_full // 4
    lax.fori_loop(0, n_quads, quad, 0)
    rest = n_full - 4 * n_quads
    odd = rest % 2

    @pl.when(rest >= 2)
    def _():
        step(4 * n_quads, 0, 1)
        step(4 * n_quads + 1, 1, 0)

    @pl.when(odd == 1)
    def _():
        step(n_full - 1, 0, 1)

    for h in range(heads):
        kv_step(h, n_full, s_ref[odd, h], masked=True)

    for h in range(heads):
        acc = acc_ref[h]
        o0 = acc[0:dh, 0:tq]
        o1 = acc[0:dh, tq:2 * tq]
        l0 = acc[dh:dh + 1, 0:tq]
        l1 = acc[dh:dh + 1, tq:2 * tq]
        o = (o0 * (1.0 / l0) - lam * (o1 * (1.0 / l1))).T
        ms = jnp.mean(o * o, axis=-1, keepdims=True)
        y = (o * lax.rsqrt(ms + EPS) * sw_ref[...]) * (1.0 - lam_init)
        g = dg_ref[:, h * dh:(h + 1) * dh]
        y = y * (g * (1.0 / (1.0 + jnp.exp(-g))))
        o_ref[:, h * dh:(h + 1) * dh] = y.astype(o_ref.dtype)


def _diff_attention(pa, pvt, pg, lq1, lk1, lq2, lk2, sw, fqc, fkc, *, lam_init, tq, tk):
    b, seq, _ = pa.shape
    dh = LANES_V7X
    h = DIFF_HEADS
    hw = h * dh
    n_k = seq // tk
    assert tk % tq == 0 and seq % tk == 0 and tq <= 256
    assert pvt.shape == (b * n_k, hw, tk)
    pvt = pvt.reshape(b, n_k, hw, tk)
    kern = functools.partial(_attn_kernel, seq=seq, tq=tq, tk=tk, heads=h, lam_init=lam_init)
    vec = lambda n: pl.BlockSpec((1, n), lambda i, j: (0, 0))
    return pl.pallas_call(
        kern,
        grid=(b, seq // tq),
        in_specs=[
            pl.BlockSpec((None, tq, hw), lambda i, j: (i, j, 0)),
            pl.BlockSpec((None, seq, hw), lambda i, j: (i, 0, 1)),
            pl.BlockSpec((None, n_k, hw, tk), lambda i, j: (i, 0, 0, 0)),
            pl.BlockSpec((None, tq, hw), lambda i, j: (i, j, 0)),
            vec(lq1.shape[1]), vec(lk1.shape[1]), vec(lq2.shape[1]), vec(lk2.shape[1]),
            vec(dh),
            pl.BlockSpec((h, 1, dh), lambda i, j: (0, 0, 0)),
            vec(dh),
        ],
        out_specs=pl.BlockSpec((None, tq, hw), lambda i, j: (i, j, 0)),
        out_shape=jax.ShapeDtypeStruct((b, seq, hw), BF16),
        scratch_shapes=[
            pltpu.VMEM((seq, dh), BF16),
            pltpu.VMEM((h, 2 * tq, 2 * dh), BF16),
            pltpu.VMEM((h, 1, 2 * tq), F32),
            pltpu.VMEM((h, dh + ONES_ROWS, 2 * tq), F32),
            pltpu.VMEM((2, h, tk, 2 * tq), F32),
        ],
        compiler_params=pltpu.CompilerParams(
            dimension_semantics=("arbitrary", "arbitrary"), vmem_limit_bytes=VMEM_LIMIT_V7X),
        name="diff_attn",
    )(pa, pa, pvt, pg, lq1, lk1, lq2, lk2, sw, fqc, fkc)


def _gla_kernel(q_ref, k_ref, v_ref, gg_ref, ga_ref, wgu_ref, bg_ref, nw_ref, o_ref, st_ref,
                *, lb, chunk, heads):
    dk = q_ref.shape[1] // heads
    dv = v_ref.shape[1] // heads
    n_chunks = lb // chunk

    @pl.when(pl.program_id(1) == 0)
    def _():
        st_ref[...] = jnp.zeros(st_ref.shape, F32)

    z = jnp.dot(ga_ref[...], wgu_ref[...], preferred_element_type=F32) + bg_ref[...]
    g = (jnp.minimum(z, 0.0) - jnp.log(1.0 + jnp.exp(-jnp.abs(z)))) * (1.0 / GLA_GATE_NORMALIZER)

    sb = min(lb, GLA_SUB_BLOCK)
    n_sub = lb // sb
    r = lax.broadcasted_iota(jnp.int32, (sb, sb), 0)
    c = lax.broadcasted_iota(jnp.int32, (sb, sb), 1)
    causal = ((r // chunk) == (c // chunk)) & (c <= r)
    ltri = jnp.where(causal, 1.0, 0.0).astype(BF16)
    g_hi = g.astype(BF16)
    g_lo = (g - g_hi.astype(F32)).astype(BF16)
    ghl = jnp.concatenate([g_hi, g_lo], axis=1)
    bb = jnp.concatenate([jnp.dot(ltri, ghl[s * sb:(s + 1) * sb], preferred_element_type=F32)
                          for s in range(n_sub)], axis=0)
    nk = heads * dk
    b_cum = bb[:, :nk] + bb[:, nk:]

    q = q_ref[...].astype(F32) * (dk ** -0.5)
    k = k_ref[...].astype(F32)
    q_in = (q * jnp.exp(b_cum)).astype(BF16)
    k_in_f = k * jnp.exp(-b_cum)
    k_in = k_in_f.astype(BF16)
    decay = [jnp.exp(b_cum[(ci + 1) * chunk - 1:(ci + 1) * chunk, :]) for ci in range(n_chunks)]
    k_st = jnp.concatenate(
        [k_in_f[ci * chunk:(ci + 1) * chunk] * decay[ci] for ci in range(n_chunks)],
        axis=0).astype(BF16)

    head_of_lane = lax.broadcasted_iota(jnp.int32, (lb, nk), 1) // dk
    zero_bf = jnp.zeros((lb, nk), BF16)

    o_intra = []
    k_st_h = []
    for h in range(heads):
        sel = head_of_lane == h
        q_h = jnp.where(sel, q_in, zero_bf)
        parts = []
        for s in range(n_sub):
            rows = slice(s * sb, (s + 1) * sb)
            a = lax.dot_general(q_h[rows], k_in[rows], NT_DIMS, preferred_element_type=F32)
            a = jnp.where(causal, a, 0.0).astype(BF16)
            parts.append(jnp.dot(a, v_ref[rows, h * dv:(h + 1) * dv], preferred_element_type=F32))
        o_intra.append(jnp.concatenate(parts, axis=0))
        k_st_h.append(jnp.where(sel, k_st, zero_bf))

    kv_t = []
    for ci in range(n_chunks):
        rows = slice(ci * chunk, (ci + 1) * chunk)
        kv_t.append(jnp.concatenate(
            [lax.dot_general(v_ref[rows, h * dv:(h + 1) * dv], k_st_h[h][rows], TN_DIMS,
                             preferred_element_type=F32) for h in range(heads)], axis=0))
    o_inter = []
    st = st_ref[...]
    for ci in range(n_chunks):
        rows = slice(ci * chunk, (ci + 1) * chunk)
        o_inter.append(lax.dot_general(q_in[rows], st.astype(BF16), NT_DIMS, preferred_element_type=F32))
        st = st * decay[ci] + kv_t[ci]
    st_ref[...] = st
    o_inter = jnp.concatenate(o_inter, axis=0)

    for h in range(heads):
        cols = slice(h * dv, (h + 1) * dv)
        o = o_intra[h] + o_inter[:, cols]
        ms = jnp.mean(o * o, axis=-1, keepdims=True)
        y = o * lax.rsqrt(ms + EPS) * nw_ref[...]
        gate = gg_ref[:, cols]
        y = y * (gate * (1.0 / (1.0 + jnp.exp(-gate))))
        o_ref[:, cols] = y.astype(o_ref.dtype)


def _gla(pa, pg, pr, wgu, bg, nw, *, lb):
    b, seq, _ = pa.shape
    heads = GLA_HEADS
    nk = wgu.shape[1]
    nv = pg.shape[2] // 2
    a_q = 3 * DIFF_HEADS * LANES_V7X
    assert a_q % nk == 0 and (a_q + 2 * nk) % nv == 0 and seq % lb == 0 and lb % GLA_CHUNK == 0
    kern = functools.partial(_gla_kernel, lb=lb, chunk=GLA_CHUNK, heads=heads)
    return pl.pallas_call(
        kern,
        grid=(b, seq // lb),
        in_specs=[
            pl.BlockSpec((None, lb, nk), lambda i, t: (i, t, a_q // nk)),
            pl.BlockSpec((None, lb, nk), lambda i, t: (i, t, a_q // nk + 1)),
            pl.BlockSpec((None, lb, nv), lambda i, t: (i, t, (a_q + 2 * nk) // nv)),
            pl.BlockSpec((None, lb, nv), lambda i, t: (i, t, 1)),
            pl.BlockSpec((None, lb, pr.shape[2]), lambda i, t: (i, t, 0)),
            pl.BlockSpec(wgu.shape, lambda i, t: (0, 0)),
            pl.BlockSpec((1, nk), lambda i, t: (0, 0)),
            pl.BlockSpec((1, nv // heads), lambda i, t: (0, 0)),
        ],
        out_specs=pl.BlockSpec((None, lb, nv), lambda i, t: (i, t, 0)),
        out_shape=jax.ShapeDtypeStruct((b, seq, nv), BF16),
        scratch_shapes=[pltpu.VMEM((nv, nk), F32)],
        compiler_params=pltpu.CompilerParams(
            dimension_semantics=("arbitrary", "arbitrary"), vmem_limit_bytes=VMEM_LIMIT_V7X),
        name="gla",
    )(pa, pa, pa, pg, pr, wgu, bg, nw)


def _outproj_kernel(x_ref, ya_ref, yb_ref, w_ref, fw_ref, o_ref, *, final_norm):
    na = ya_ref.shape[1]
    mix = jnp.dot(ya_ref[...], w_ref[:na, :], preferred_element_type=F32)
    mix = mix + jnp.dot(yb_ref[...], w_ref[na:, :], preferred_element_type=F32)
    hres = x_ref[...] + mix
    if final_norm:
        ms = jnp.mean(hres * hres, axis=-1, keepdims=True)
        hres = hres * lax.rsqrt(ms + EPS) * fw_ref[...]
    o_ref[...] = hres


def _outproj(x2, ya, yb, w, fw, *, final_norm, tm):
    n, d = x2.shape
    na, nb = ya.shape[1], yb.shape[1]
    kern = functools.partial(_outproj_kernel, final_norm=final_norm)
    return pl.pallas_call(
        kern,
        grid=(n // tm,),
        in_specs=[
            pl.BlockSpec((tm, d), lambda i: (i, 0)),
            pl.BlockSpec((tm, na), lambda i: (i, 0)),
            pl.BlockSpec((tm, nb), lambda i: (i, 0)),
            pl.BlockSpec((na + nb, d), lambda i: (0, 0)),
            pl.BlockSpec((1, d), lambda i: (0, 0)),
        ],
        out_specs=pl.BlockSpec((tm, d), lambda i: (i, 0)),
        out_shape=jax.ShapeDtypeStruct((n, d), F32),
        compiler_params=pltpu.CompilerParams(
            dimension_semantics=("arbitrary",), vmem_limit_bytes=VMEM_LIMIT_V7X),
        name="outproj",
    )(x2, ya, yb, w, fw)


def _block_sizes(seq):
    return dict(tm_in=min(512, seq), tq=min(256, seq), tk=min(512, seq),
                lb=min(512, seq), tm_out=min(1024, seq))


def _layer(x, norm_w, w_in, w_gate_up, b_gate, lq1, lk1, lq2, lk2, subln_w, gla_norm_w, w_out,
           final_w, layer_idx, final_norm):
    b, seq, d = x.shape
    bs = _block_sizes(seq)
    d_mix = w_out.shape[0]
    diff_w = d_mix // 2
    gla_w = d_mix - diff_w
    qk_cols = DIFF_HEADS * LANES_V7X
    gla_qk = w_gate_up.shape[1]
    rank = w_gate_up.shape[0]
    o_dq, o_dk, o_dv = 0, qk_cols, 2 * qk_cols
    o_dg = o_dv + diff_w
    o_gq = o_dg + diff_w
    o_gk = o_gq + gla_qk
    o_gv = o_gk + gla_qk
    o_gg = o_gv + gla_w
    o_ga = o_gg + gla_w
    assert w_in.shape[1] == o_ga + rank
    wa = jnp.concatenate([w_in[:, o_dq:o_dg], w_in[:, o_gq:o_gg]], axis=1).astype(BF16)
    wg = jnp.concatenate([w_in[:, o_dg:o_gq], w_in[:, o_gg:o_ga]], axis=1).astype(BF16)
    wr = jnp.pad(w_in[:, o_ga:], ((0, 0), (0, LANES_V7X - rank))).astype(BF16)
    wgu = jnp.pad(w_gate_up, ((0, LANES_V7X - rank), (0, 0))).astype(BF16)

    d_half = LANES_V7X // 2
    x2 = x.reshape(b * seq, d)
    assert bs["tm_in"] == bs["tk"] and diff_w == qk_cols
    pa, pg, pr, pvt = _inproj(x2, norm_w.reshape(1, d), wa, wg, wr,
                              q_cols=qk_cols, q_scale=(d_half ** -0.5) * LOG2E, tm=bs["tm_in"])
    pa = pa.reshape(b, seq, -1)
    pg = pg.reshape(b, seq, -1)
    pr = pr.reshape(b, seq, -1)

    lam_init = _lambda_init(layer_idx)
    fqc, fkc = _alibi_feature_consts(DIFF_HEADS)
    row = lambda v: v.reshape(1, -1).astype(F32)
    ya = _diff_attention(pa, pvt, pg, row(lq1), row(lk1), row(lq2), row(lk2), row(subln_w), fqc, fkc,
                         lam_init=lam_init, tq=bs["tq"], tk=bs["tk"])
    yb = _gla(pa, pg, pr, wgu, row(b_gate), row(gla_norm_w), lb=bs["lb"])

    out = _outproj(x2, ya.reshape(b * seq, -1), yb.reshape(b * seq, -1), w_out.astype(BF16),
                   row(final_w), final_norm=final_norm, tm=bs["tm_out"])
    return out.reshape(b, seq, d)


def kernel(x, norm_w, w_in, w_gate_up, b_gate, lambda_q1, lambda_k1, lambda_q2, lambda_k2,
           diff_subln_w, gla_norm_w, w_out, final_norm_w):
    depth = norm_w.shape[0]
    h = x
    for layer in range(depth):
        h = _layer(h, norm_w[layer], w_in[layer], w_gate_up[layer], b_gate[layer],
                   lambda_q1[layer], lambda_k1[layer], lambda_q2[layer], lambda_k2[layer],
                   diff_subln_w[layer], gla_norm_w[layer], w_out[layer], final_norm_w,
                   layer, final_norm=(layer == depth - 1))
    return h
```

```python
import functools
import math

import numpy as np
import jax
import jax.numpy as jnp
from jax import lax
from jax.experimental import pallas as pl
from jax.experimental.pallas import tpu as pltpu

EPS = 1e-6
DIFF_HEADS = 4
GLA_HEADS = 4
GLA_GATE_NORMALIZER = 16.0
GLA_CHUNK = 64
GLA_SUB_BLOCK = 256
LOG2E = math.log2(math.e)

LANES_V7X = 128
VMEM_BYTES_V7X = 64 * 1024 * 1024
VMEM_LIMIT_V7X = VMEM_BYTES_V7X - 8 * 1024 * 1024
BF16_EXACT_INT_BITS = 8

F32 = jnp.float32
BF16 = jnp.bfloat16
NT_DIMS = (((1,), (1,)), ((), ()))
TN_DIMS = (((0,), (0,)), ((), ()))

N_SPLIT = 3
ONES_ROWS = 16


def _lambda_init(layer_idx):
    return 0.8 - 0.6 * math.exp(-0.3 * layer_idx)


def _alibi_feature_consts(n_heads):
    slopes = np.exp2(-8.0 * np.arange(1, n_heads + 1, dtype=np.float64) / n_heads)
    per = n_heads * N_SPLIT
    fq = np.zeros((n_heads, 1, LANES_V7X), np.float32)
    fk = np.zeros((1, LANES_V7X), np.float32)
    for h in range(n_heads):
        rem = np.float32(slopes[h] * LOG2E)
        for x in range(N_SPLIT):
            p = np.float32(rem).astype(BF16).astype(np.float32)
            rem = np.float32(rem - p)
            fk[0, h * N_SPLIT + x] = -p
            fk[0, per + h * N_SPLIT + x] = -p
            fq[h, 0, 2 * per + x] = p
            fq[h, 0, 2 * per + N_SPLIT + x] = p
    return jnp.asarray(fq), jnp.asarray(fk)


def _pos_features(pos, lane, first_hi, first_lo):
    hi = (pos >> BF16_EXACT_INT_BITS) << BF16_EXACT_INT_BITS
    lo = pos & ((1 << BF16_EXACT_INT_BITS) - 1)
    in_hi = (lane >= first_hi) & (lane < first_hi + N_SPLIT)
    in_lo = (lane >= first_lo) & (lane < first_lo + N_SPLIT)
    return jnp.where(in_hi, hi, jnp.where(in_lo, lo, 0)).astype(F32)


def _inproj_kernel(x_ref, nw_ref, wa_ref, wg_ref, wr_ref, oa_ref, og_ref, or_ref, ovt_ref,
                   *, q_cols, q_scale):
    x = x_ref[...]
    ms = jnp.mean(x * x, axis=-1, keepdims=True)
    h = (x * lax.rsqrt(ms + EPS) * nw_ref[...]).astype(BF16)
    acc = jnp.dot(h, wa_ref[...], preferred_element_type=F32)
    oa_ref[:, :q_cols] = (acc[:, :q_cols] * q_scale).astype(BF16)
    oa_ref[:, q_cols:] = acc[:, q_cols:].astype(BF16)
    ovt_ref[...] = acc[:, 2 * q_cols:3 * q_cols].T.astype(BF16)
    og_ref[...] = jnp.dot(h, wg_ref[...], preferred_element_type=F32)
    or_ref[...] = jnp.dot(h, wr_ref[...], preferred_element_type=F32).astype(BF16)


def _inproj(x2, nw, wa, wg, wr, *, q_cols, q_scale, tm):
    n, d = x2.shape
    na, ng, nr = wa.shape[1], wg.shape[1], wr.shape[1]
    kern = functools.partial(_inproj_kernel, q_cols=q_cols, q_scale=q_scale)
    return pl.pallas_call(
        kern,
        grid=(n // tm,),
        in_specs=[
            pl.BlockSpec((tm, d), lambda i: (i, 0)),
            pl.BlockSpec((1, d), lambda i: (0, 0)),
            pl.BlockSpec((d, na), lambda i: (0, 0)),
            pl.BlockSpec((d, ng), lambda i: (0, 0)),
            pl.BlockSpec((d, nr), lambda i: (0, 0)),
        ],
        out_specs=[
            pl.BlockSpec((tm, na), lambda i: (i, 0)),
            pl.BlockSpec((tm, ng), lambda i: (i, 0)),
            pl.BlockSpec((tm, nr), lambda i: (i, 0)),
            pl.BlockSpec((None, q_cols, tm), lambda i: (i, 0, 0)),
        ],
        out_shape=[
            jax.ShapeDtypeStruct((n, na), BF16),
            jax.ShapeDtypeStruct((n, ng), F32),
            jax.ShapeDtypeStruct((n, nr), BF16),
            jax.ShapeDtypeStruct((n // tm, q_cols, tm), BF16),
        ],
        compiler_params=pltpu.CompilerParams(
            dimension_semantics=("arbitrary",), vmem_limit_bytes=VMEM_LIMIT_V7X),
        name="inproj",
    )(x2, nw, wa, wg, wr)


def _attn_kernel(q_ref, k_ref, vt_ref, dg_ref, lq1_ref, lk1_ref, lq2_ref, lk2_ref, sw_ref,
                 fqc_ref, fkc_ref, o_ref, kf_ref, qp_ref, m_ref, acc_ref, s_ref,
                 *, seq, tq, tk, heads, lam_init):
    dh = LANES_V7X
    half = dh // 2
    n_k = seq // tk
    qi = pl.program_id(1)
    qstart = pl.multiple_of(qi * tq, tq)

    @pl.when((pl.program_id(0) == 0) & (qi == 0))
    def _():
        fkc = fkc_ref[...]
        lane_k = lax.broadcasted_iota(jnp.int32, (tk, dh), 1)
        row_k = lax.broadcasted_iota(jnp.int32, (tk, dh), 0)
        per = heads * N_SPLIT

        def build_kf(j, carry):
            start = pl.multiple_of(j * tk, tk)
            f = _pos_features(row_k + start, lane_k, 2 * per, 2 * per + N_SPLIT) + fkc
            kf_ref[pl.ds(start, tk), :] = f.astype(BF16)
            return carry

        lax.fori_loop(0, n_k, build_kf, 0)

    lam = (jnp.exp(jnp.sum(lq1_ref[...] * lk1_ref[...], axis=-1, keepdims=True))
           - jnp.exp(jnp.sum(lq2_ref[...] * lk2_ref[...], axis=-1, keepdims=True))
           + lam_init)

    lane_q = lax.broadcasted_iota(jnp.int32, (tq, dh), 1)
    row_q = lax.broadcasted_iota(jnp.int32, (tq, dh), 0)
    for h in range(heads):
        q = q_ref[:, h * dh:(h + 1) * dh]
        zero = jnp.zeros_like(q)
        fq = (_pos_features(row_q + qstart, lane_q, h * N_SPLIT, (heads + h) * N_SPLIT)
              + fqc_ref[h]).astype(BF16)
        qp_ref[h, 0:tq, 0:dh] = jnp.where(lane_q < half, q, zero)
        qp_ref[h, tq:2 * tq, 0:dh] = jnp.where(lane_q >= half, q, zero)
        qp_ref[h, 0:tq, dh:2 * dh] = fq
        qp_ref[h, tq:2 * tq, dh:2 * dh] = fq
    m_ref[...] = jnp.full(m_ref.shape, -jnp.inf, F32)
    acc_ref[...] = jnp.zeros(acc_ref.shape, F32)

    def ones_rows(rows):
        first = lax.broadcasted_iota(jnp.int32, (ONES_ROWS, rows), 0) == 0
        return jnp.where(first, 1.0, 0.0).astype(BF16)

    def causal_bias(rows, off):
        r2 = lax.broadcasted_iota(jnp.int32, (rows, 2 * tq), 0)
        c2 = lax.broadcasted_iota(jnp.int32, (rows, 2 * tq), 1)
        return jnp.where(r2 - jnp.where(c2 >= tq, c2 - tq, c2) <= off, 0.0, -jnp.inf)

    diag_off = qstart - (qstart // tk) * tk

    def scores(h, j):
        kstart = pl.multiple_of(j * tk, tk)
        kp = jnp.concatenate([k_ref[pl.ds(kstart, tk), h * dh:(h + 1) * dh],
                              kf_ref[pl.ds(kstart, tk), :]], axis=1)
        return lax.dot_general(kp, qp_ref[h], NT_DIMS, preferred_element_type=F32)

    def kv_step(h, j, s, mask_bias=None):
        rows = s.shape[0]
        if mask_bias is not None:
            s = s + mask_bias
        m_old = m_ref[h]
        m_new = jnp.maximum(m_old, jnp.max(s, axis=0, keepdims=True))
        alpha = jnp.exp2(m_old - m_new)
        p = jnp.exp2(s - m_new).astype(BF16)
        vt = jnp.concatenate([vt_ref[j, h * dh:(h + 1) * dh, 0:rows], ones_rows(rows)], axis=0)
        pv = jnp.dot(vt, p, preferred_element_type=F32)
        acc_ref[h] = alpha * acc_ref[h] + pv
        m_ref[h] = m_new

    n_full = qstart // tk

    def step(j, src, dst):
        s_all = [s_ref[src, h] for h in range(heads)]
        for h in range(heads):
            s_ref[dst, h] = scores(h, j + 1)
        for h in range(heads):
            kv_step(h, j, s_all[h])

    for h in range(heads):
        s_ref[0, h] = scores(h, 0)

    def quad(i, c):
        for u in range(4):
            step(4 * i + u, u % 2, 1 - u % 2)
        return c

    n_quads = n_full // 4
    lax.fori_loop(0, n_quads, quad, 0)
    rest = n_full - 4 * n_quads
    odd = rest % 2

    @pl.when(rest >= 2)
    def _():
        step(4 * n_quads, 0, 1)
        step(4 * n_quads + 1, 1, 0)

    @pl.when(odd == 1)
    def _():
        step(n_full - 1, 0, 1)

    def finish(off):
        rows = off + tq
        mask_bias = causal_bias(rows, off)
        for h in range(heads):
            kv_step(h, n_full, s_ref[odd, h, 0:rows, :], mask_bias)
        for h in range(heads):
            acc = acc_ref[h]
            o0 = acc[0:dh, 0:tq]
            o1 = acc[0:dh, tq:2 * tq]
            l0 = acc[dh:dh + 1, 0:tq]
            l1 = acc[dh:dh + 1, tq:2 * tq]
            o = (o0 * (1.0 / l0) - lam * (o1 * (1.0 / l1))).T
            ms = jnp.mean(o * o, axis=-1, keepdims=True)
            y = (o * lax.rsqrt(ms + EPS) * sw_ref[...]) * (1.0 - lam_init)
            g = dg_ref[:, h * dh:(h + 1) * dh]
            y = y * (g * (1.0 / (1.0 + jnp.exp(-g))))
            o_ref[:, h * dh:(h + 1) * dh] = y.astype(o_ref.dtype)

    for off in range(0, tk, tq):
        pl.when(diag_off == off)(functools.partial(finish, off))


def _diff_attention(pa, pvt, pg, lq1, lk1, lq2, lk2, sw, fqc, fkc, *, lam_init, tq, tk):
    b, seq, _ = pa.shape
    dh = LANES_V7X
    h = DIFF_HEADS
    hw = h * dh
    n_k = seq // tk
    assert tk % tq == 0 and seq % tk == 0 and tq <= 256
    assert seq <= 1 << (2 * BF16_EXACT_INT_BITS)
    assert pvt.shape == (b * n_k, hw, tk)
    pvt = pvt.reshape(b, n_k, hw, tk)
    kern = functools.partial(_attn_kernel, seq=seq, tq=tq, tk=tk, heads=h, lam_init=lam_init)
    vec = lambda n: pl.BlockSpec((1, n), lambda i, j: (0, 0))
    return pl.pallas_call(
        kern,
        grid=(b, seq // tq),
        in_specs=[
            pl.BlockSpec((None, tq, hw), lambda i, j: (i, j, 0)),
            pl.BlockSpec((None, seq, hw), lambda i, j: (i, 0, 1)),
            pl.BlockSpec((None, n_k, hw, tk), lambda i, j: (i, 0, 0, 0)),
            pl.BlockSpec((None, tq, hw), lambda i, j: (i, j, 0)),
            vec(lq1.shape[1]), vec(lk1.shape[1]), vec(lq2.shape[1]), vec(lk2.shape[1]),
            vec(dh),
            pl.BlockSpec((h, 1, dh), lambda i, j: (0, 0, 0)),
            vec(dh),
        ],
        out_specs=pl.BlockSpec((None, tq, hw), lambda i, j: (i, j, 0)),
        out_shape=jax.ShapeDtypeStruct((b, seq, hw), BF16),
        scratch_shapes=[
            pltpu.VMEM((seq, dh), BF16),
            pltpu.VMEM((h, 2 * tq, 2 * dh), BF16),
            pltpu.VMEM((h, 1, 2 * tq), F32),
            pltpu.VMEM((h, dh + ONES_ROWS, 2 * tq), F32),
            pltpu.VMEM((2, h, tk, 2 * tq), F32),
        ],
        compiler_params=pltpu.CompilerParams(
            dimension_semantics=("arbitrary", "arbitrary"), vmem_limit_bytes=VMEM_LIMIT_V7X),
        name="diff_attn",
    )(pa, pa, pvt, pg, lq1, lk1, lq2, lk2, sw, fqc, fkc)


def _gla_kernel(q_ref, k_ref, v_ref, gg_ref, ga_ref, wgu_ref, bg_ref, nw_ref, o_ref, st_ref,
                *, lb, chunk, heads):
    dk = q_ref.shape[1] // heads
    dv = v_ref.shape[1] // heads
    n_chunks = lb // chunk

    @pl.when(pl.program_id(1) == 0)
    def _():
        st_ref[...] = jnp.zeros(st_ref.shape, F32)

    z = jnp.dot(ga_ref[...], wgu_ref[...], preferred_element_type=F32) + bg_ref[...]
    g = (jnp.minimum(z, 0.0) - jnp.log(1.0 + jnp.exp(-jnp.abs(z)))) * (1.0 / GLA_GATE_NORMALIZER)

    sb = min(lb, GLA_SUB_BLOCK)
    n_sub = lb // sb
    r = lax.broadcasted_iota(jnp.int32, (sb, sb), 0)
    c = lax.broadcasted_iota(jnp.int32, (sb, sb), 1)
    causal = ((r // chunk) == (c // chunk)) & (c <= r)
    ltri = jnp.where(causal, 1.0, 0.0).astype(BF16)
    g_hi = g.astype(BF16)
    g_lo = (g - g_hi.astype(F32)).astype(BF16)
    ghl = jnp.concatenate([g_hi, g_lo], axis=1)
    bb = jnp.concatenate([jnp.dot(ltri, ghl[s * sb:(s + 1) * sb], preferred_element_type=F32)
                          for s in range(n_sub)], axis=0)
    nk = heads * dk
    b_cum = bb[:, :nk] + bb[:, nk:]

    q = q_ref[...].astype(F32) * (dk ** -0.5)
    k = k_ref[...].astype(F32)
    q_in = (q * jnp.exp(b_cum)).astype(BF16)
    k_in_f = k * jnp.exp(-b_cum)
    k_in = k_in_f.astype(BF16)
    decay = [jnp.exp(b_cum[(ci + 1) * chunk - 1:(ci + 1) * chunk, :]) for ci in range(n_chunks)]
    k_st = jnp.concatenate(
        [k_in_f[ci * chunk:(ci + 1) * chunk] * decay[ci] for ci in range(n_chunks)],
        axis=0).astype(BF16)

    head_of_lane = lax.broadcasted_iota(jnp.int32, (lb, nk), 1) // dk
    zero_bf = jnp.zeros((lb, nk), BF16)

    o_intra = []
    k_st_h = []
    for h in range(heads):
        sel = head_of_lane == h
        q_h = jnp.where(sel, q_in, zero_bf)
        parts = []
        for s in range(n_sub):
            rows = slice(s * sb, (s + 1) * sb)
            a = lax.dot_general(q_h[rows], k_in[rows], NT_DIMS, preferred_element_type=F32)
            a = jnp.where(causal, a, 0.0).astype(BF16)
            parts.append(jnp.dot(a, v_ref[rows, h * dv:(h + 1) * dv], preferred_element_type=F32))
        o_intra.append(jnp.concatenate(parts, axis=0))
        k_st_h.append(jnp.where(sel, k_st, zero_bf))

    kv_t = []
    for ci in range(n_chunks):
        rows = slice(ci * chunk, (ci + 1) * chunk)
        kv_t.append(jnp.concatenate(
            [lax.dot_general(v_ref[rows, h * dv:(h + 1) * dv], k_st_h[h][rows], TN_DIMS,
                             preferred_element_type=F32) for h in range(heads)], axis=0))
    o_inter = []
    st = st_ref[...]
    for ci in range(n_chunks):
        rows = slice(ci * chunk, (ci + 1) * chunk)
        o_inter.append(lax.dot_general(q_in[rows], st.astype(BF16), NT_DIMS, preferred_element_type=F32))
        st = st * decay[ci] + kv_t[ci]
    st_ref[...] = st
    o_inter = jnp.concatenate(o_inter, axis=0)

    for h in range(heads):
        cols = slice(h * dv, (h + 1) * dv)
        o = o_intra[h] + o_inter[:, cols]
        ms = jnp.mean(o * o, axis=-1, keepdims=True)
        y = o * lax.rsqrt(ms + EPS) * nw_ref[...]
        gate = gg_ref[:, cols]
        y = y * (gate * (1.0 / (1.0 + jnp.exp(-gate))))
        o_ref[:, cols] = y.astype(o_ref.dtype)


def _gla(pa, pg, pr, wgu, bg, nw, *, lb):
    b, seq, _ = pa.shape
    heads = GLA_HEADS
    nk = wgu.shape[1]
    nv = pg.shape[2] // 2
    a_q = 3 * DIFF_HEADS * LANES_V7X
    assert a_q % nk == 0 and (a_q + 2 * nk) % nv == 0 and seq % lb == 0 and lb % GLA_CHUNK == 0
    kern = functools.partial(_gla_kernel, lb=lb, chunk=GLA_CHUNK, heads=heads)
    return pl.pallas_call(
        kern,
        grid=(b, seq // lb),
        in_specs=[
            pl.BlockSpec((None, lb, nk), lambda i, t: (i, t, a_q // nk)),
            pl.BlockSpec((None, lb, nk), lambda i, t: (i, t, a_q // nk + 1)),
            pl.BlockSpec((None, lb, nv), lambda i, t: (i, t, (a_q + 2 * nk) // nv)),
            pl.BlockSpec((None, lb, nv), lambda i, t: (i, t, 1)),
            pl.BlockSpec((None, lb, pr.shape[2]), lambda i, t: (i, t, 0)),
            pl.BlockSpec(wgu.shape, lambda i, t: (0, 0)),
            pl.BlockSpec((1, nk), lambda i, t: (0, 0)),
            pl.BlockSpec((1, nv // heads), lambda i, t: (0, 0)),
        ],
        out_specs=pl.BlockSpec((None, lb, nv), lambda i, t: (i, t, 0)),
        out_shape=jax.ShapeDtypeStruct((b, seq, nv), BF16),
        scratch_shapes=[pltpu.VMEM((nv, nk), F32)],
        compiler_params=pltpu.CompilerParams(
            dimension_semantics=("arbitrary", "arbitrary"), vmem_limit_bytes=VMEM_LIMIT_V7X),
        name="gla",
    )(pa, pa, pa, pg, pr, wgu, bg, nw)


def _outproj_kernel(x_ref, ya_ref, yb_ref, w_ref, fw_ref, o_ref, *, final_norm):
    na = ya_ref.shape[1]
    mix = jnp.dot(ya_ref[...], w_ref[:na, :], preferred_element_type=F32)
    mix = mix + jnp.dot(yb_ref[...], w_ref[na:, :], preferred_element_type=F32)
    hres = x_ref[...] + mix
    if final_norm:
        ms = jnp.mean(hres * hres, axis=-1, keepdims=True)
        hres = hres * lax.rsqrt(ms + EPS) * fw_ref[...]
    o_ref[...] = hres


def _outproj(x2, ya, yb, w, fw, *, final_norm, tm):
    n, d = x2.shape
    na, nb = ya.shape[1], yb.shape[1]
    kern = functools.partial(_outproj_kernel, final_norm=final_norm)
    return pl.pallas_call(
        kern,
        grid=(n // tm,),
        in_specs=[
            pl.BlockSpec((tm, d), lambda i: (i, 0)),
            pl.BlockSpec((tm, na), lambda i: (i, 0)),
            pl.BlockSpec((tm, nb), lambda i: (i, 0)),
            pl.BlockSpec((na + nb, d), lambda i: (0, 0)),
            pl.BlockSpec((1, d), lambda i: (0, 0)),
        ],
        out_specs=pl.BlockSpec((tm, d), lambda i: (i, 0)),
        out_shape=jax.ShapeDtypeStruct((n, d), F32),
        compiler_params=pltpu.CompilerParams(
            dimension_semantics=("arbitrary",), vmem_limit_bytes=VMEM_LIMIT_V7X),
        name="outproj",
    )(x2, ya, yb, w, fw)


def _block_sizes(seq):
    return dict(tm_in=min(512, seq), tq=min(256, seq), tk=min(512, seq),
                lb=min(512, seq), tm_out=min(1024, seq))


def _layer(x, norm_w, w_in, w_gate_up, b_gate, lq1, lk1, lq2, lk2, subln_w, gla_norm_w, w_out,
           final_w, layer_idx, final_norm):
    b, seq, d = x.shape
    bs = _block_sizes(seq)
    d_mix = w_out.shape[0]
    diff_w = d_mix // 2
    gla_w = d_mix - diff_w
    qk_cols = DIFF_HEADS * LANES_V7X
    gla_qk = w_gate_up.shape[1]
    rank = w_gate_up.shape[0]
    o_dq, o_dk, o_dv = 0, qk_cols, 2 * qk_cols
    o_dg = o_dv + diff_w
    o_gq = o_dg + diff_w
    o_gk = o_gq + gla_qk
    o_gv = o_gk + gla_qk
    o_gg = o_gv + gla_w
    o_ga = o_gg + gla_w
    assert w_in.shape[1] == o_ga + rank
    wa = jnp.concatenate([w_in[:, o_dq:o_dg], w_in[:, o_gq:o_gg]], axis=1).astype(BF16)
    wg = jnp.concatenate([w_in[:, o_dg:o_gq], w_in[:, o_gg:o_ga]], axis=1).astype(BF16)
    wr = jnp.pad(w_in[:, o_ga:], ((0, 0), (0, LANES_V7X - rank))).astype(BF16)
    wgu = jnp.pad(w_gate_up, ((0, LANES_V7X - rank), (0, 0))).astype(BF16)

    d_half = LANES_V7X // 2
    x2 = x.reshape(b * seq, d)
    assert bs["tm_in"] == bs["tk"] and diff_w == qk_cols
    pa, pg, pr, pvt = _inproj(x2, norm_w.reshape(1, d), wa, wg, wr,
                              q_cols=qk_cols, q_scale=(d_half ** -0.5) * LOG2E, tm=bs["tm_in"])
    pa = pa.reshape(b, seq, -1)
    pg = pg.reshape(b, seq, -1)
    pr = pr.reshape(b, seq, -1)

    lam_init = _lambda_init(layer_idx)
    fqc, fkc = _alibi_feature_consts(DIFF_HEADS)
    row = lambda v: v.reshape(1, -1).astype(F32)
    ya = _diff_attention(pa, pvt, pg, row(lq1), row(lk1), row(lq2), row(lk2), row(subln_w), fqc, fkc,
                         lam_init=lam_init, tq=bs["tq"], tk=bs["tk"])
    yb = _gla(pa, pg, pr, wgu, row(b_gate), row(gla_norm_w), lb=bs["lb"])

    out = _outproj(x2, ya.reshape(b * seq, -1), yb.reshape(b * seq, -1), w_out.astype(BF16),
                   row(final_w), final_norm=final_norm, tm=bs["tm_out"])
    return out.reshape(b, seq, d)


def kernel(x, norm_w, w_in, w_gate_up, b_gate, lambda_q1, lambda_k1, lambda_q2, lambda_k2,
           diff_subln_w, gla_norm_w, w_out, final_norm_w):
    depth = norm_w.shape[0]
    h = x
    for layer in range(depth):
        h = _layer(h, norm_w[layer], w_in[layer], w_gate_up[layer], b_gate[layer],
                   lambda_q1[layer], lambda_k1[layer], lambda_q2[layer], lambda_k2[layer],
                   diff_subln_w[layer], gla_norm_w[layer], w_out[layer], final_norm_w,
                   layer, final_norm=(layer == depth - 1))
    return h
```

```python
import functools
import math

import numpy as np
import jax
import jax.numpy as jnp
from jax import lax
from jax.experimental import pallas as pl
from jax.experimental.pallas import tpu as pltpu

EPS = 1e-6
DIFF_HEADS = 4
GLA_HEADS = 4
GLA_GATE_NORMALIZER = 16.0
GLA_CHUNK = 64
GLA_SUB_BLOCK = 256
LOG2E = math.log2(math.e)

LANES_V7X = 128
VMEM_BYTES_V7X = 64 * 1024 * 1024
VMEM_LIMIT_V7X = VMEM_BYTES_V7X - 8 * 1024 * 1024
BF16_EXACT_INT_BITS = 8

F32 = jnp.float32
BF16 = jnp.bfloat16
NT_DIMS = (((1,), (1,)), ((), ()))
TN_DIMS = (((0,), (0,)), ((), ()))

N_SPLIT = 3
ONES_ROWS = 16


def _lambda_init(layer_idx):
    return 0.8 - 0.6 * math.exp(-0.3 * layer_idx)


def _alibi_feature_consts(n_heads):
    slopes = np.exp2(-8.0 * np.arange(1, n_heads + 1, dtype=np.float64) / n_heads)
    per = n_heads * N_SPLIT
    fq = np.zeros((n_heads, 1, LANES_V7X), np.float32)
    fk = np.zeros((1, LANES_V7X), np.float32)
    for h in range(n_heads):
        rem = np.float32(slopes[h] * LOG2E)
        for x in range(N_SPLIT):
            p = np.float32(rem).astype(BF16).astype(np.float32)
            rem = np.float32(rem - p)
            fk[0, h * N_SPLIT + x] = -p
            fk[0, per + h * N_SPLIT + x] = -p
            fq[h, 0, 2 * per + x] = p
            fq[h, 0, 2 * per + N_SPLIT + x] = p
    return jnp.asarray(fq), jnp.asarray(fk)


def _pos_features(pos, lane, first_hi, first_lo):
    hi = (pos >> BF16_EXACT_INT_BITS) << BF16_EXACT_INT_BITS
    lo = pos & ((1 << BF16_EXACT_INT_BITS) - 1)
    in_hi = (lane >= first_hi) & (lane < first_hi + N_SPLIT)
    in_lo = (lane >= first_lo) & (lane < first_lo + N_SPLIT)
    return jnp.where(in_hi, hi, jnp.where(in_lo, lo, 0)).astype(F32)


def _inproj_kernel(x_ref, nw_ref, wa_ref, wg_ref, wr_ref, oa_ref, og_ref, or_ref, ovt_ref,
                   *, q_cols, q_scale):
    x = x_ref[...]
    ms = jnp.mean(x * x, axis=-1, keepdims=True)
    h = (x * lax.rsqrt(ms + EPS) * nw_ref[...]).astype(BF16)
    acc = jnp.dot(h, wa_ref[...], preferred_element_type=F32)
    oa_ref[:, :q_cols] = (acc[:, :q_cols] * q_scale).astype(BF16)
    oa_ref[:, q_cols:] = acc[:, q_cols:].astype(BF16)
    ovt_ref[...] = acc[:, 2 * q_cols:3 * q_cols].T.astype(BF16)
    og_ref[...] = jnp.dot(h, wg_ref[...], preferred_element_type=F32)
    or_ref[...] = jnp.dot(h, wr_ref[...], preferred_element_type=F32).astype(BF16)


def _inproj(x2, nw, wa, wg, wr, *, q_cols, q_scale, tm):
    n, d = x2.shape
    na, ng, nr = wa.shape[1], wg.shape[1], wr.shape[1]
    kern = functools.partial(_inproj_kernel, q_cols=q_cols, q_scale=q_scale)
    return pl.pallas_call(
        kern,
        grid=(n // tm,),
        in_specs=[
            pl.BlockSpec((tm, d), lambda i: (i, 0)),
            pl.BlockSpec((1, d), lambda i: (0, 0)),
            pl.BlockSpec((d, na), lambda i: (0, 0)),
            pl.BlockSpec((d, ng), lambda i: (0, 0)),
            pl.BlockSpec((d, nr), lambda i: (0, 0)),
        ],
        out_specs=[
            pl.BlockSpec((tm, na), lambda i: (i, 0)),
            pl.BlockSpec((tm, ng), lambda i: (i, 0)),
            pl.BlockSpec((tm, nr), lambda i: (i, 0)),
            pl.BlockSpec((None, q_cols, tm), lambda i: (i, 0, 0)),
        ],
        out_shape=[
            jax.ShapeDtypeStruct((n, na), BF16),
            jax.ShapeDtypeStruct((n, ng), F32),
            jax.ShapeDtypeStruct((n, nr), BF16),
            jax.ShapeDtypeStruct((n // tm, q_cols, tm), BF16),
        ],
        compiler_params=pltpu.CompilerParams(
            dimension_semantics=("arbitrary",), vmem_limit_bytes=VMEM_LIMIT_V7X),
        name="inproj",
    )(x2, nw, wa, wg, wr)


def _attn_kernel(q_ref, k_ref, vt_ref, dg_ref, lq1_ref, lk1_ref, lq2_ref, lk2_ref, sw_ref,
                 fqc_ref, fkc_ref, o_ref, kf_ref, qp_ref, m_ref, acc_ref, s_ref,
                 *, seq, tq, tk, heads, lam_init):
    dh = LANES_V7X
    half = dh // 2
    n_k = seq // tk
    qi = pl.program_id(1)
    qstart = pl.multiple_of(qi * tq, tq)

    @pl.when((pl.program_id(0) == 0) & (qi == 0))
    def _():
        fkc = fkc_ref[...]
        lane_k = lax.broadcasted_iota(jnp.int32, (tk, dh), 1)
        row_k = lax.broadcasted_iota(jnp.int32, (tk, dh), 0)
        per = heads * N_SPLIT

        def build_kf(j, carry):
            start = pl.multiple_of(j * tk, tk)
            f = _pos_features(row_k + start, lane_k, 2 * per, 2 * per + N_SPLIT) + fkc
            kf_ref[pl.ds(start, tk), :] = f.astype(BF16)
            return carry

        lax.fori_loop(0, n_k, build_kf, 0)

    lam = (jnp.exp(jnp.sum(lq1_ref[...] * lk1_ref[...], axis=-1, keepdims=True))
           - jnp.exp(jnp.sum(lq2_ref[...] * lk2_ref[...], axis=-1, keepdims=True))
           + lam_init)

    lane_q = lax.broadcasted_iota(jnp.int32, (tq, dh), 1)
    row_q = lax.broadcasted_iota(jnp.int32, (tq, dh), 0)
    for h in range(heads):
        q = q_ref[:, h * dh:(h + 1) * dh]
        zero = jnp.zeros_like(q)
        fq = (_pos_features(row_q + qstart, lane_q, h * N_SPLIT, (heads + h) * N_SPLIT)
              + fqc_ref[h]).astype(BF16)
        qp_ref[h, 0:tq, 0:dh] = jnp.where(lane_q < half, q, zero)
        qp_ref[h, tq:2 * tq, 0:dh] = jnp.where(lane_q >= half, q, zero)
        qp_ref[h, 0:tq, dh:2 * dh] = fq
        qp_ref[h, tq:2 * tq, dh:2 * dh] = fq
    m_ref[...] = jnp.full(m_ref.shape, -jnp.inf, F32)
    acc_ref[...] = jnp.zeros(acc_ref.shape, F32)

    def ones_rows(rows):
        first = lax.broadcasted_iota(jnp.int32, (ONES_ROWS, rows), 0) == 0
        return jnp.where(first, 1.0, 0.0).astype(BF16)

    def causal_bias(rows, off):
        r2 = lax.broadcasted_iota(jnp.int32, (rows, 2 * tq), 0)
        c2 = lax.broadcasted_iota(jnp.int32, (rows, 2 * tq), 1)
        return jnp.where(r2 - jnp.where(c2 >= tq, c2 - tq, c2) <= off, 0.0, -jnp.inf)

    diag_off = qstart - (qstart // tk) * tk

    def scores(h, j):
        kstart = pl.multiple_of(j * tk, tk)
        kp = jnp.concatenate([k_ref[pl.ds(kstart, tk), h * dh:(h + 1) * dh],
                              kf_ref[pl.ds(kstart, tk), :]], axis=1)
        return lax.dot_general(kp, qp_ref[h], NT_DIMS, preferred_element_type=F32)

    def kv_step(h, j, s, mask_bias=None):
        rows = s.shape[0]
        if mask_bias is not None:
            s = s + mask_bias
        m_old = m_ref[h]
        m_new = jnp.maximum(m_old, jnp.max(s, axis=0, keepdims=True))
        alpha = jnp.exp2(m_old - m_new)
        p = jnp.exp2(s - m_new).astype(BF16)
        vt = jnp.concatenate([vt_ref[j, h * dh:(h + 1) * dh, 0:rows], ones_rows(rows)], axis=0)
        pv = jnp.dot(vt, p, preferred_element_type=F32)
        acc_ref[h] = alpha * acc_ref[h] + pv
        m_ref[h] = m_new

    n_full = qstart // tk

    def step(j, src, dst):
        s_all = [s_ref[src, h] for h in range(heads)]
        for h in range(heads):
            s_ref[dst, h] = scores(h, j + 1)
        for h in range(heads):
            kv_step(h, j, s_all[h])

    for h in range(heads):
        s_ref[0, h] = scores(h, 0)

    def quad(i, c):
        for u in range(4):
            step(4 * i + u, u % 2, 1 - u % 2)
        return c

    n_quads = n_full // 4
    lax.fori_loop(0, n_quads, quad, 0)
    rest = n_full - 4 * n_quads
    odd = rest % 2

    @pl.when(rest >= 2)
    def _():
        step(4 * n_quads, 0, 1)
        step(4 * n_quads + 1, 1, 0)

    @pl.when(odd == 1)
    def _():
        step(n_full - 1, 0, 1)

    def finish(off):
        rows = off + tq
        mask_bias = causal_bias(rows, off)
        for h in range(heads):
            kv_step(h, n_full, s_ref[odd, h, 0:rows, :], mask_bias)
        for h in range(heads):
            acc = acc_ref[h]
            o0 = acc[0:dh, 0:tq]
            o1 = acc[0:dh, tq:2 * tq]
            l0 = acc[dh:dh + 1, 0:tq]
            l1 = acc[dh:dh + 1, tq:2 * tq]
            o = (o0 * (1.0 / l0) - lam * (o1 * (1.0 / l1))).T
            ms = jnp.mean(o * o, axis=-1, keepdims=True)
            y = (o * lax.rsqrt(ms + EPS) * sw_ref[...]) * (1.0 - lam_init)
            g = dg_ref[:, h * dh:(h + 1) * dh]
            y = y * (g * (1.0 / (1.0 + jnp.exp(-g))))
            o_ref[:, h * dh:(h + 1) * dh] = y.astype(o_ref.dtype)

    for off in range(0, tk, tq):
        pl.when(diag_off == off)(functools.partial(finish, off))


def _diff_attention(pa, pvt, pg, lq1, lk1, lq2, lk2, sw, fqc, fkc, *, lam_init, tq, tk):
    b, seq, _ = pa.shape
    dh = LANES_V7X
    h = DIFF_HEADS
    hw = h * dh
    n_k = seq // tk
    assert tk % tq == 0 and seq % tk == 0 and tq <= 256
    assert seq <= 1 << (2 * BF16_EXACT_INT_BITS)
    assert pvt.shape == (b * n_k, hw, tk)
    pvt = pvt.reshape(b, n_k, hw, tk)
    kern = functools.partial(_attn_kernel, seq=seq, tq=tq, tk=tk, heads=h, lam_init=lam_init)
    vec = lambda n: pl.BlockSpec((1, n), lambda i, j: (0, 0))
    return pl.pallas_call(
        kern,
        grid=(b, seq // tq),
        in_specs=[
            pl.BlockSpec((None, tq, hw), lambda i, j: (i, j, 0)),
            pl.BlockSpec((None, seq, hw), lambda i, j: (i, 0, 1)),
            pl.BlockSpec((None, n_k, hw, tk), lambda i, j: (i, 0, 0, 0)),
            pl.BlockSpec((None, tq, hw), lambda i, j: (i, j, 0)),
            vec(lq1.shape[1]), vec(lk1.shape[1]), vec(lq2.shape[1]), vec(lk2.shape[1]),
            vec(dh),
            pl.BlockSpec((h, 1, dh), lambda i, j: (0, 0, 0)),
            vec(dh),
        ],
        out_specs=pl.BlockSpec((None, tq, hw), lambda i, j: (i, j, 0)),
        out_shape=jax.ShapeDtypeStruct((b, seq, hw), BF16),
        scratch_shapes=[
            pltpu.VMEM((seq, dh), BF16),
            pltpu.VMEM((h, 2 * tq, 2 * dh), BF16),
            pltpu.VMEM((h, 1, 2 * tq), F32),
            pltpu.VMEM((h, dh + ONES_ROWS, 2 * tq), F32),
            pltpu.VMEM((2, h, tk, 2 * tq), F32),
        ],
        compiler_params=pltpu.CompilerParams(
            dimension_semantics=("arbitrary", "arbitrary"), vmem_limit_bytes=VMEM_LIMIT_V7X),
        name="diff_attn",
    )(pa, pa, pvt, pg, lq1, lk1, lq2, lk2, sw, fqc, fkc)


def _gla_kernel(q_ref, k_ref, v_ref, gg_ref, ga_ref, wgu_ref, bg_ref, nw_ref, o_ref, st_ref,
                *, lb, chunk, heads):
    dk = q_ref.shape[1] // heads
    dv = v_ref.shape[1] // heads
    n_chunks = lb // chunk

    @pl.when(pl.program_id(1) == 0)
    def _():
        st_ref[...] = jnp.zeros(st_ref.shape, F32)

    z = jnp.dot(ga_ref[...], wgu_ref[...], preferred_element_type=F32) + bg_ref[...]
    g = (jnp.minimum(z, 0.0) - jnp.log(1.0 + jnp.exp(-jnp.abs(z)))) * (1.0 / GLA_GATE_NORMALIZER)

    sb = min(lb, GLA_SUB_BLOCK)
    n_sub = lb // sb
    r = lax.broadcasted_iota(jnp.int32, (sb, sb), 0)
    c = lax.broadcasted_iota(jnp.int32, (sb, sb), 1)
    causal = ((r // chunk) == (c // chunk)) & (c <= r)
    ltri = jnp.where(causal, 1.0, 0.0).astype(BF16)
    g_hi = g.astype(BF16)
    g_lo = (g - g_hi.astype(F32)).astype(BF16)
    ghl = jnp.concatenate([g_hi, g_lo], axis=1)
    bb = jnp.concatenate([jnp.dot(ltri, ghl[s * sb:(s + 1) * sb], preferred_element_type=F32)
                          for s in range(n_sub)], axis=0)
    nk = heads * dk
    b_cum = bb[:, :nk] + bb[:, nk:]

    q = q_ref[...].astype(F32) * (dk ** -0.5)
    k = k_ref[...].astype(F32)
    q_in = (q * jnp.exp(b_cum)).astype(BF16)
    k_in_f = k * jnp.exp(-b_cum)
    k_in = k_in_f.astype(BF16)
    decay = [jnp.exp(b_cum[(ci + 1) * chunk - 1:(ci + 1) * chunk, :]) for ci in range(n_chunks)]
    k_st = jnp.concatenate(
        [k_in_f[ci * chunk:(ci + 1) * chunk] * decay[ci] for ci in range(n_chunks)],
        axis=0).astype(BF16)

    head_of_lane = lax.broadcasted_iota(jnp.int32, (lb, nk), 1) // dk
    zero_bf = jnp.zeros((lb, nk), BF16)

    o_intra = []
    k_st_h = []
    for h in range(heads):
        sel = head_of_lane == h
        q_h = jnp.where(sel, q_in, zero_bf)
        parts = []
        for s in range(n_sub):
            rows = slice(s * sb, (s + 1) * sb)
            a = lax.dot_general(q_h[rows], k_in[rows], NT_DIMS, preferred_element_type=F32)
            a = jnp.where(causal, a, 0.0).astype(BF16)
            parts.append(jnp.dot(a, v_ref[rows, h * dv:(h + 1) * dv], preferred_element_type=F32))
        o_intra.append(jnp.concatenate(parts, axis=0))
        k_st_h.append(jnp.where(sel, k_st, zero_bf))

    kv_t = []
    for ci in range(n_chunks):
        rows = slice(ci * chunk, (ci + 1) * chunk)
        kv_t.append(jnp.concatenate(
            [lax.dot_general(v_ref[rows, h * dv:(h + 1) * dv], k_st_h[h][rows], TN_DIMS,
                             preferred_element_type=F32) for h in range(heads)], axis=0))
    o_inter = []
    st = st_ref[...]
    for ci in range(n_chunks):
        rows = slice(ci * chunk, (ci + 1) * chunk)
        o_inter.append(lax.dot_general(q_in[rows], st.astype(BF16), NT_DIMS, preferred_element_type=F32))
        st = st * decay[ci] + kv_t[ci]
    st_ref[...] = st
    o_inter = jnp.concatenate(o_inter, axis=0)

    for h in range(heads):
        cols = slice(h * dv, (h + 1) * dv)
        o = o_intra[h] + o_inter[:, cols]
        ms = jnp.mean(o * o, axis=-1, keepdims=True)
        y = o * lax.rsqrt(ms + EPS) * nw_ref[...]
        gate = gg_ref[:, cols]
        y = y * (gate * (1.0 / (1.0 + jnp.exp(-gate))))
        o_ref[:, cols] = y.astype(o_ref.dtype)


def _gla(pa, pg, pr, wgu, bg, nw, *, lb):
    b, seq, _ = pa.shape
    heads = GLA_HEADS
    nk = wgu.shape[1]
    nv = pg.shape[2] // 2
    a_q = 3 * DIFF_HEADS * LANES_V7X
    assert a_q % nk == 0 and (a_q + 2 * nk) % nv == 0 and seq % lb == 0 and lb % GLA_CHUNK == 0
    kern = functools.partial(_gla_kernel, lb=lb, chunk=GLA_CHUNK, heads=heads)
    return pl.pallas_call(
        kern,
        grid=(b, seq // lb),
        in_specs=[
            pl.BlockSpec((None, lb, nk), lambda i, t: (i, t, a_q // nk)),
            pl.BlockSpec((None, lb, nk), lambda i, t: (i, t, a_q // nk + 1)),
            pl.BlockSpec((None, lb, nv), lambda i, t: (i, t, (a_q + 2 * nk) // nv)),
            pl.BlockSpec((None, lb, nv), lambda i, t: (i, t, 1)),
            pl.BlockSpec((None, lb, pr.shape[2]), lambda i, t: (i, t, 0)),
            pl.BlockSpec(wgu.shape, lambda i, t: (0, 0)),
            pl.BlockSpec((1, nk), lambda i, t: (0, 0)),
            pl.BlockSpec((1, nv // heads), lambda i, t: (0, 0)),
        ],
        out_specs=pl.BlockSpec((None, lb, nv), lambda i, t: (i, t, 0)),
        out_shape=jax.ShapeDtypeStruct((b, seq, nv), BF16),
        scratch_shapes=[pltpu.VMEM((nv, nk), F32)],
        compiler_params=pltpu.CompilerParams(
            dimension_semantics=("arbitrary", "arbitrary"), vmem_limit_bytes=VMEM_LIMIT_V7X),
        name="gla",
    )(pa, pa, pa, pg, pr, wgu, bg, nw)


def _outproj_kernel(x_ref, ya_ref, yb_ref, w_ref, fw_ref, o_ref, *, final_norm):
    na = ya_ref.shape[1]
    mix = jnp.dot(ya_ref[...], w_ref[:na, :], preferred_element_type=F32)
    mix = mix + jnp.dot(yb_ref[...], w_ref[na:, :], preferred_element_type=F32)
    hres = x_ref[...] + mix
    if final_norm:
        ms = jnp.mean(hres * hres, axis=-1, keepdims=True)
        hres = hres * lax.rsqrt(ms + EPS) * fw_ref[...]
    o_ref[...] = hres


def _outproj(x2, ya, yb, w, fw, *, final_norm, tm):
    n, d = x2.shape
    na, nb = ya.shape[1], yb.shape[1]
    kern = functools.partial(_outproj_kernel, final_norm=final_norm)
    return pl.pallas_call(
        kern,
        grid=(n // tm,),
        in_specs=[
            pl.BlockSpec((tm, d), lambda i: (i, 0)),
            pl.BlockSpec((tm, na), lambda i: (i, 0)),
            pl.BlockSpec((tm, nb), lambda i: (i, 0)),
            pl.BlockSpec((na + nb, d), lambda i: (0, 0)),
            pl.BlockSpec((1, d), lambda i: (0, 0)),
        ],
        out_specs=pl.BlockSpec((tm, d), lambda i: (i, 0)),
        out_shape=jax.ShapeDtypeStruct((n, d), F32),
        compiler_params=pltpu.CompilerParams(
            dimension_semantics=("arbitrary",), vmem_limit_bytes=VMEM_LIMIT_V7X),
        name="outproj",
    )(x2, ya, yb, w, fw)


def _block_sizes(seq):
    return dict(tm_in=min(512, seq), tq=min(256, seq), tk=min(512, seq),
                lb=min(1024, seq), tm_out=min(1024, seq))


def _layer(x, norm_w, w_in, w_gate_up, b_gate, lq1, lk1, lq2, lk2, subln_w, gla_norm_w, w_out,
           final_w, layer_idx, final_norm):
    b, seq, d = x.shape
    bs = _block_sizes(seq)
    d_mix = w_out.shape[0]
    diff_w = d_mix // 2
    gla_w = d_mix - diff_w
    qk_cols = DIFF_HEADS * LANES_V7X
    gla_qk = w_gate_up.shape[1]
    rank = w_gate_up.shape[0]
    o_dq, o_dk, o_dv = 0, qk_cols, 2 * qk_cols
    o_dg = o_dv + diff_w
    o_gq = o_dg + diff_w
    o_gk = o_gq + gla_qk
    o_gv = o_gk + gla_qk
    o_gg = o_gv + gla_w
    o_ga = o_gg + gla_w
    assert w_in.shape[1] == o_ga + rank
    wa = jnp.concatenate([w_in[:, o_dq:o_dg], w_in[:, o_gq:o_gg]], axis=1).astype(BF16)
    wg = jnp.concatenate([w_in[:, o_dg:o_gq], w_in[:, o_gg:o_ga]], axis=1).astype(BF16)
    wr = jnp.pad(w_in[:, o_ga:], ((0, 0), (0, LANES_V7X - rank))).astype(BF16)
    wgu = jnp.pad(w_gate_up, ((0, LANES_V7X - rank), (0, 0))).astype(BF16)

    d_half = LANES_V7X // 2
    x2 = x.reshape(b * seq, d)
    assert bs["tm_in"] == bs["tk"] and diff_w == qk_cols
    pa, pg, pr, pvt = _inproj(x2, norm_w.reshape(1, d), wa, wg, wr,
                              q_cols=qk_cols, q_scale=(d_half ** -0.5) * LOG2E, tm=bs["tm_in"])
    pa = pa.reshape(b, seq, -1)
    pg = pg.reshape(b, seq, -1)
    pr = pr.reshape(b, seq, -1)

    lam_init = _lambda_init(layer_idx)
    fqc, fkc = _alibi_feature_consts(DIFF_HEADS)
    row = lambda v: v.reshape(1, -1).astype(F32)
    ya = _diff_attention(pa, pvt, pg, row(lq1), row(lk1), row(lq2), row(lk2), row(subln_w), fqc, fkc,
                         lam_init=lam_init, tq=bs["tq"], tk=bs["tk"])
    yb = _gla(pa, pg, pr, wgu, row(b_gate), row(gla_norm_w), lb=bs["lb"])

    out = _outproj(x2, ya.reshape(b * seq, -1), yb.reshape(b * seq, -1), w_out.astype(BF16),
                   row(final_w), final_norm=final_norm, tm=bs["tm_out"])
    return out.reshape(b, seq, d)


def kernel(x, norm_w, w_in, w_gate_up, b_gate, lambda_q1, lambda_k1, lambda_q2, lambda_k2,
           diff_subln_w, gla_norm_w, w_out, final_norm_w):
    depth = norm_w.shape[0]
    h = x
    for layer in range(depth):
        h = _layer(h, norm_w[layer], w_in[layer], w_gate_up[layer], b_gate[layer],
                   lambda_q1[layer], lambda_k1[layer], lambda_q2[layer], lambda_k2[layer],
                   diff_subln_w[layer], gla_norm_w[layer], w_out[layer], final_norm_w,
                   layer, final_norm=(layer == depth - 1))
    return h
```

```python
import functools
import math

import numpy as np
import jax
import jax.numpy as jnp
from jax import lax
from jax.experimental import pallas as pl
from jax.experimental.pallas import tpu as pltpu

EPS = 1e-6
DIFF_HEADS = 4
GLA_HEADS = 4
GLA_GATE_NORMALIZER = 16.0
GLA_CHUNK = 64
GLA_SUB_BLOCK = 256
LOG2E = math.log2(math.e)

LANES_V7X = 128
VMEM_BYTES_V7X = 64 * 1024 * 1024
VMEM_LIMIT_V7X = VMEM_BYTES_V7X - 8 * 1024 * 1024
BF16_EXACT_INT_BITS = 8

F32 = jnp.float32
BF16 = jnp.bfloat16
NT_DIMS = (((1,), (1,)), ((), ()))
TN_DIMS = (((0,), (0,)), ((), ()))

N_SPLIT = 3
ONES_ROWS = 16


def _lambda_init(layer_idx):
    return 0.8 - 0.6 * math.exp(-0.3 * layer_idx)


def _alibi_feature_consts(n_heads):
    slopes = np.exp2(-8.0 * np.arange(1, n_heads + 1, dtype=np.float64) / n_heads)
    per = n_heads * N_SPLIT
    fq = np.zeros((n_heads, 1, LANES_V7X), np.float32)
    fk = np.zeros((1, LANES_V7X), np.float32)
    for h in range(n_heads):
        rem = np.float32(slopes[h] * LOG2E)
        for x in range(N_SPLIT):
            p = np.float32(rem).astype(BF16).astype(np.float32)
            rem = np.float32(rem - p)
            fk[0, h * N_SPLIT + x] = -p
            fk[0, per + h * N_SPLIT + x] = -p
            fq[h, 0, 2 * per + x] = p
            fq[h, 0, 2 * per + N_SPLIT + x] = p
    return jnp.asarray(fq), jnp.asarray(fk)


def _pos_features(pos, lane, first_hi, first_lo):
    hi = (pos >> BF16_EXACT_INT_BITS) << BF16_EXACT_INT_BITS
    lo = pos & ((1 << BF16_EXACT_INT_BITS) - 1)
    in_hi = (lane >= first_hi) & (lane < first_hi + N_SPLIT)
    in_lo = (lane >= first_lo) & (lane < first_lo + N_SPLIT)
    return jnp.where(in_hi, hi, jnp.where(in_lo, lo, 0)).astype(F32)


def _inproj_kernel(x_ref, nw_ref, wa_ref, wg_ref, wr_ref, oa_ref, og_ref, or_ref, ovt_ref,
                   *, q_cols, q_scale):
    x = x_ref[...]
    ms = jnp.mean(x * x, axis=-1, keepdims=True)
    h = (x * lax.rsqrt(ms + EPS) * nw_ref[...]).astype(BF16)
    acc = jnp.dot(h, wa_ref[...], preferred_element_type=F32)
    oa_ref[:, :q_cols] = (acc[:, :q_cols] * q_scale).astype(BF16)
    oa_ref[:, q_cols:] = acc[:, q_cols:].astype(BF16)
    ovt_ref[...] = acc[:, 2 * q_cols:3 * q_cols].T.astype(BF16)
    og_ref[...] = jnp.dot(h, wg_ref[...], preferred_element_type=F32)
    or_ref[...] = jnp.dot(h, wr_ref[...], preferred_element_type=F32).astype(BF16)


def _inproj(x2, nw, wa, wg, wr, *, q_cols, q_scale, tm):
    n, d = x2.shape
    na, ng, nr = wa.shape[1], wg.shape[1], wr.shape[1]
    kern = functools.partial(_inproj_kernel, q_cols=q_cols, q_scale=q_scale)
    return pl.pallas_call(
        kern,
        grid=(n // tm,),
        in_specs=[
            pl.BlockSpec((tm, d), lambda i: (i, 0)),
            pl.BlockSpec((1, d), lambda i: (0, 0)),
            pl.BlockSpec((d, na), lambda i: (0, 0)),
            pl.BlockSpec((d, ng), lambda i: (0, 0)),
            pl.BlockSpec((d, nr), lambda i: (0, 0)),
        ],
        out_specs=[
            pl.BlockSpec((tm, na), lambda i: (i, 0)),
            pl.BlockSpec((tm, ng), lambda i: (i, 0)),
            pl.BlockSpec((tm, nr), lambda i: (i, 0)),
            pl.BlockSpec((None, q_cols, tm), lambda i: (i, 0, 0)),
        ],
        out_shape=[
            jax.ShapeDtypeStruct((n, na), BF16),
            jax.ShapeDtypeStruct((n, ng), F32),
            jax.ShapeDtypeStruct((n, nr), BF16),
            jax.ShapeDtypeStruct((n // tm, q_cols, tm), BF16),
        ],
        compiler_params=pltpu.CompilerParams(
            dimension_semantics=("arbitrary",), vmem_limit_bytes=VMEM_LIMIT_V7X),
        name="inproj",
    )(x2, nw, wa, wg, wr)


def _attn_kernel(q_ref, k_ref, vt_ref, dg_ref, lq1_ref, lk1_ref, lq2_ref, lk2_ref, sw_ref,
                 fqc_ref, fkc_ref, o_ref, kf_ref, qp_ref, m_ref, acc_ref, s_ref,
                 *, seq, tq, tk, heads, lam_init):
    dh = LANES_V7X
    half = dh // 2
    n_k = seq // tk
    qi = pl.program_id(1)
    qstart = pl.multiple_of(qi * tq, tq)

    @pl.when((pl.program_id(0) == 0) & (qi == 0))
    def _():
        fkc = fkc_ref[...]
        lane_k = lax.broadcasted_iota(jnp.int32, (tk, dh), 1)
        row_k = lax.broadcasted_iota(jnp.int32, (tk, dh), 0)
        per = heads * N_SPLIT

        def build_kf(j, carry):
            start = pl.multiple_of(j * tk, tk)
            f = _pos_features(row_k + start, lane_k, 2 * per, 2 * per + N_SPLIT) + fkc
            kf_ref[pl.ds(start, tk), :] = f.astype(BF16)
            return carry

        lax.fori_loop(0, n_k, build_kf, 0)

    lam = (jnp.exp(jnp.sum(lq1_ref[...] * lk1_ref[...], axis=-1, keepdims=True))
           - jnp.exp(jnp.sum(lq2_ref[...] * lk2_ref[...], axis=-1, keepdims=True))
           + lam_init)

    lane_q = lax.broadcasted_iota(jnp.int32, (tq, dh), 1)
    row_q = lax.broadcasted_iota(jnp.int32, (tq, dh), 0)
    for h in range(heads):
        q = q_ref[:, h * dh:(h + 1) * dh]
        zero = jnp.zeros_like(q)
        fq = (_pos_features(row_q + qstart, lane_q, h * N_SPLIT, (heads + h) * N_SPLIT)
              + fqc_ref[h]).astype(BF16)
        qp_ref[h, 0:tq, 0:dh] = jnp.where(lane_q < half, q, zero)
        qp_ref[h, tq:2 * tq, 0:dh] = jnp.where(lane_q >= half, q, zero)
        qp_ref[h, 0:tq, dh:2 * dh] = fq
        qp_ref[h, tq:2 * tq, dh:2 * dh] = fq
    m_ref[...] = jnp.full(m_ref.shape, -jnp.inf, F32)
    acc_ref[...] = jnp.zeros(acc_ref.shape, F32)

    def ones_rows(rows):
        first = lax.broadcasted_iota(jnp.int32, (ONES_ROWS, rows), 0) == 0
        return jnp.where(first, 1.0, 0.0).astype(BF16)

    def causal_bias(rows, off):
        r2 = lax.broadcasted_iota(jnp.int32, (rows, 2 * tq), 0)
        c2 = lax.broadcasted_iota(jnp.int32, (rows, 2 * tq), 1)
        return jnp.where(r2 - jnp.where(c2 >= tq, c2 - tq, c2) <= off, 0.0, -jnp.inf)

    diag_off = qstart - (qstart // tk) * tk

    def scores(h, j):
        kstart = pl.multiple_of(j * tk, tk)
        kp = jnp.concatenate([k_ref[pl.ds(kstart, tk), h * dh:(h + 1) * dh],
                              kf_ref[pl.ds(kstart, tk), :]], axis=1)
        return lax.dot_general(kp, qp_ref[h], NT_DIMS, preferred_element_type=F32)

    def kv_step(h, j, s, mask_bias=None):
        rows = s.shape[0]
        if mask_bias is not None:
            s = s + mask_bias
        m_old = m_ref[h]
        m_new = jnp.maximum(m_old, jnp.max(s, axis=0, keepdims=True))
        alpha = jnp.exp2(m_old - m_new)
        p = jnp.exp2(s - m_new).astype(BF16)
        vt = jnp.concatenate([vt_ref[j, h * dh:(h + 1) * dh, 0:rows], ones_rows(rows)], axis=0)
        pv = jnp.dot(vt, p, preferred_element_type=F32)
        acc_ref[h] = alpha * acc_ref[h] + pv
        m_ref[h] = m_new

    n_full = qstart // tk

    def step(j, src, dst):
        s_all = [s_ref[src, h] for h in range(heads)]
        for h in range(heads):
            s_ref[dst, h] = scores(h, j + 1)
        for h in range(heads):
            kv_step(h, j, s_all[h])

    for h in range(heads):
        s_ref[0, h] = scores(h, 0)

    def quad(i, c):
        for u in range(4):
            step(4 * i + u, u % 2, 1 - u % 2)
        return c

    n_quads = n_full // 4
    lax.fori_loop(0, n_quads, quad, 0)
    rest = n_full - 4 * n_quads
    odd = rest % 2

    @pl.when(rest >= 2)
    def _():
        step(4 * n_quads, 0, 1)
        step(4 * n_quads + 1, 1, 0)

    @pl.when(odd == 1)
    def _():
        step(n_full - 1, 0, 1)

    def finish(off):
        rows = off + tq
        mask_bias = causal_bias(rows, off)
        for h in range(heads):
            kv_step(h, n_full, s_ref[odd, h, 0:rows, :], mask_bias)
        for h in range(heads):
            acc = acc_ref[h]
            o0 = acc[0:dh, 0:tq]
            o1 = acc[0:dh, tq:2 * tq]
            l0 = acc[dh:dh + 1, 0:tq]
            l1 = acc[dh:dh + 1, tq:2 * tq]
            o = (o0 * (1.0 / l0) - lam * (o1 * (1.0 / l1))).T
            ms = jnp.mean(o * o, axis=-1, keepdims=True)
            y = (o * lax.rsqrt(ms + EPS) * sw_ref[...]) * (1.0 - lam_init)
            g = dg_ref[:, h * dh:(h + 1) * dh]
            y = y * (g * (1.0 / (1.0 + jnp.exp(-g))))
            o_ref[:, h * dh:(h + 1) * dh] = y.astype(o_ref.dtype)

    for off in range(0, tk, tq):
        pl.when(diag_off == off)(functools.partial(finish, off))


def _diff_attention(pa, pvt, pg, lq1, lk1, lq2, lk2, sw, fqc, fkc, *, lam_init, tq, tk):
    b, seq, _ = pa.shape
    dh = LANES_V7X
    h = DIFF_HEADS
    hw = h * dh
    n_k = seq // tk
    assert tk % tq == 0 and seq % tk == 0 and tq <= 256
    assert seq <= 1 << (2 * BF16_EXACT_INT_BITS)
    assert pvt.shape == (b * n_k, hw, tk)
    pvt = pvt.reshape(b, n_k, hw, tk)
    kern = functools.partial(_attn_kernel, seq=seq, tq=tq, tk=tk, heads=h, lam_init=lam_init)
    vec = lambda n: pl.BlockSpec((1, n), lambda i, j: (0, 0))
    return pl.pallas_call(
        kern,
        grid=(b, seq // tq),
        in_specs=[
            pl.BlockSpec((None, tq, hw), lambda i, j: (i, j, 0)),
            pl.BlockSpec((None, seq, hw), lambda i, j: (i, 0, 1)),
            pl.BlockSpec((None, n_k, hw, tk), lambda i, j: (i, 0, 0, 0)),
            pl.BlockSpec((None, tq, hw), lambda i, j: (i, j, 0)),
            vec(lq1.shape[1]), vec(lk1.shape[1]), vec(lq2.shape[1]), vec(lk2.shape[1]),
            vec(dh),
            pl.BlockSpec((h, 1, dh), lambda i, j: (0, 0, 0)),
            vec(dh),
        ],
        out_specs=pl.BlockSpec((None, tq, hw), lambda i, j: (i, j, 0)),
        out_shape=jax.ShapeDtypeStruct((b, seq, hw), BF16),
        scratch_shapes=[
            pltpu.VMEM((seq, dh), BF16),
            pltpu.VMEM((h, 2 * tq, 2 * dh), BF16),
            pltpu.VMEM((h, 1, 2 * tq), F32),
            pltpu.VMEM((h, dh + ONES_ROWS, 2 * tq), F32),
            pltpu.VMEM((2, h, tk, 2 * tq), F32),
        ],
        compiler_params=pltpu.CompilerParams(
            dimension_semantics=("arbitrary", "arbitrary"), vmem_limit_bytes=VMEM_LIMIT_V7X),
        name="diff_attn",
    )(pa, pa, pvt, pg, lq1, lk1, lq2, lk2, sw, fqc, fkc)


def _gla_kernel(q_ref, k_ref, v_ref, gg_ref, ga_ref, wgu_ref, bg_ref, nw_ref, o_ref, st_ref,
                *, lb, chunk, heads):
    dk = q_ref.shape[1] // heads
    dv = v_ref.shape[1] // heads
    n_chunks = lb // chunk

    @pl.when(pl.program_id(1) == 0)
    def _():
        st_ref[...] = jnp.zeros(st_ref.shape, F32)

    z = jnp.dot(ga_ref[...], wgu_ref[...], preferred_element_type=F32) + bg_ref[...]
    g = (jnp.minimum(z, 0.0) - jnp.log(1.0 + jnp.exp(-jnp.abs(z)))) * (1.0 / GLA_GATE_NORMALIZER)

    sb = min(lb, GLA_SUB_BLOCK)
    n_sub = lb // sb
    r = lax.broadcasted_iota(jnp.int32, (sb, sb), 0)
    c = lax.broadcasted_iota(jnp.int32, (sb, sb), 1)
    causal = ((r // chunk) == (c // chunk)) & (c <= r)
    ltri = jnp.where(causal, 1.0, 0.0).astype(BF16)
    g_hi = g.astype(BF16)
    g_lo = (g - g_hi.astype(F32)).astype(BF16)
    ghl = jnp.concatenate([g_hi, g_lo], axis=1)
    bb = jnp.concatenate([jnp.dot(ltri, ghl[s * sb:(s + 1) * sb], preferred_element_type=F32)
                          for s in range(n_sub)], axis=0)
    nk = heads * dk
    b_cum = bb[:, :nk] + bb[:, nk:]

    q = q_ref[...].astype(F32) * (dk ** -0.5)
    k = k_ref[...].astype(F32)
    q_in = (q * jnp.exp(b_cum)).astype(BF16)
    k_in_f = k * jnp.exp(-b_cum)
    k_in = k_in_f.astype(BF16)
    decay = [jnp.exp(b_cum[(ci + 1) * chunk - 1:(ci + 1) * chunk, :]) for ci in range(n_chunks)]
    k_st = jnp.concatenate(
        [k_in_f[ci * chunk:(ci + 1) * chunk] * decay[ci] for ci in range(n_chunks)],
        axis=0).astype(BF16)

    head_of_lane = lax.broadcasted_iota(jnp.int32, (lb, nk), 1) // dk
    zero_bf = jnp.zeros((lb, nk), BF16)

    o_intra = []
    k_st_h = []
    for h in range(heads):
        sel = head_of_lane == h
        q_h = jnp.where(sel, q_in, zero_bf)
        parts = []
        for s in range(n_sub):
            rows = slice(s * sb, (s + 1) * sb)
            a = lax.dot_general(q_h[rows], k_in[rows], NT_DIMS, preferred_element_type=F32)
            a = jnp.where(causal, a, 0.0).astype(BF16)
            parts.append(jnp.dot(a, v_ref[rows, h * dv:(h + 1) * dv], preferred_element_type=F32))
        o_intra.append(jnp.concatenate(parts, axis=0))
        k_st_h.append(jnp.where(sel, k_st, zero_bf))

    kv_t = []
    for ci in range(n_chunks):
        rows = slice(ci * chunk, (ci + 1) * chunk)
        kv_t.append(jnp.concatenate(
            [lax.dot_general(v_ref[rows, h * dv:(h + 1) * dv], k_st_h[h][rows], TN_DIMS,
                             preferred_element_type=F32) for h in range(heads)], axis=0))
    o_inter = []
    st = st_ref[...]
    for ci in range(n_chunks):
        rows = slice(ci * chunk, (ci + 1) * chunk)
        o_inter.append(lax.dot_general(q_in[rows], st.astype(BF16), NT_DIMS, preferred_element_type=F32))
        st = st * decay[ci] + kv_t[ci]
    st_ref[...] = st
    o_inter = jnp.concatenate(o_inter, axis=0)

    for h in range(heads):
        cols = slice(h * dv, (h + 1) * dv)
        o = o_intra[h] + o_inter[:, cols]
        ms = jnp.mean(o * o, axis=-1, keepdims=True)
        y = o * lax.rsqrt(ms + EPS) * nw_ref[...]
        gate = gg_ref[:, cols]
        y = y * (gate * (1.0 / (1.0 + jnp.exp(-gate))))
        o_ref[:, cols] = y.astype(o_ref.dtype)


def _gla(pa, pg, pr, wgu, bg, nw, *, lb):
    b, seq, _ = pa.shape
    heads = GLA_HEADS
    nk = wgu.shape[1]
    nv = pg.shape[2] // 2
    a_q = 3 * DIFF_HEADS * LANES_V7X
    assert a_q % nk == 0 and (a_q + 2 * nk) % nv == 0 and seq % lb == 0 and lb % GLA_CHUNK == 0
    kern = functools.partial(_gla_kernel, lb=lb, chunk=GLA_CHUNK, heads=heads)
    return pl.pallas_call(
        kern,
        grid=(b, seq // lb),
        in_specs=[
            pl.BlockSpec((None, lb, nk), lambda i, t: (i, t, a_q // nk)),
            pl.BlockSpec((None, lb, nk), lambda i, t: (i, t, a_q // nk + 1)),
            pl.BlockSpec((None, lb, nv), lambda i, t: (i, t, (a_q + 2 * nk) // nv)),
            pl.BlockSpec((None, lb, nv), lambda i, t: (i, t, 1)),
            pl.BlockSpec((None, lb, pr.shape[2]), lambda i, t: (i, t, 0)),
            pl.BlockSpec(wgu.shape, lambda i, t: (0, 0)),
            pl.BlockSpec((1, nk), lambda i, t: (0, 0)),
            pl.BlockSpec((1, nv // heads), lambda i, t: (0, 0)),
        ],
        out_specs=pl.BlockSpec((None, lb, nv), lambda i, t: (i, t, 0)),
        out_shape=jax.ShapeDtypeStruct((b, seq, nv), BF16),
        scratch_shapes=[pltpu.VMEM((nv, nk), F32)],
        compiler_params=pltpu.CompilerParams(
            dimension_semantics=("arbitrary", "arbitrary"), vmem_limit_bytes=VMEM_LIMIT_V7X),
        name="gla",
    )(pa, pa, pa, pg, pr, wgu, bg, nw)


def _outproj_kernel(x_ref, ya_ref, yb_ref, w_ref, fw_ref, o_ref, *, final_norm):
    na = ya_ref.shape[1]
    mix = jnp.dot(ya_ref[...], w_ref[:na, :], preferred_element_type=F32)
    mix = mix + jnp.dot(yb_ref[...], w_ref[na:, :], preferred_element_type=F32)
    hres = x_ref[...] + mix
    if final_norm:
        ms = jnp.mean(hres * hres, axis=-1, keepdims=True)
        hres = hres * lax.rsqrt(ms + EPS) * fw_ref[...]
    o_ref[...] = hres


def _outproj(x2, ya, yb, w, fw, *, final_norm, tm):
    n, d = x2.shape
    na, nb = ya.shape[1], yb.shape[1]
    kern = functools.partial(_outproj_kernel, final_norm=final_norm)
    return pl.pallas_call(
        kern,
        grid=(n // tm,),
        in_specs=[
            pl.BlockSpec((tm, d), lambda i: (i, 0)),
            pl.BlockSpec((tm, na), lambda i: (i, 0)),
            pl.BlockSpec((tm, nb), lambda i: (i, 0)),
            pl.BlockSpec((na + nb, d), lambda i: (0, 0)),
            pl.BlockSpec((1, d), lambda i: (0, 0)),
        ],
        out_specs=pl.BlockSpec((tm, d), lambda i: (i, 0)),
        out_shape=jax.ShapeDtypeStruct((n, d), F32),
        compiler_params=pltpu.CompilerParams(
            dimension_semantics=("arbitrary",), vmem_limit_bytes=VMEM_LIMIT_V7X),
        name="outproj",
    )(x2, ya, yb, w, fw)


def _block_sizes(seq):
    return dict(tm_in=min(512, seq), tq=min(256, seq), tk=min(512, seq),
                lb=min(1024, seq), tm_out=min(2048, seq))


def _layer(x, norm_w, w_in, w_gate_up, b_gate, lq1, lk1, lq2, lk2, subln_w, gla_norm_w, w_out,
           final_w, layer_idx, final_norm):
    b, seq, d = x.shape
    bs = _block_sizes(seq)
    d_mix = w_out.shape[0]
    diff_w = d_mix // 2
    gla_w = d_mix - diff_w
    qk_cols = DIFF_HEADS * LANES_V7X
    gla_qk = w_gate_up.shape[1]
    rank = w_gate_up.shape[0]
    o_dq, o_dk, o_dv = 0, qk_cols, 2 * qk_cols
    o_dg = o_dv + diff_w
    o_gq = o_dg + diff_w
    o_gk = o_gq + gla_qk
    o_gv = o_gk + gla_qk
    o_gg = o_gv + gla_w
    o_ga = o_gg + gla_w
    assert w_in.shape[1] == o_ga + rank
    wa = jnp.concatenate([w_in[:, o_dq:o_dg], w_in[:, o_gq:o_gg]], axis=1).astype(BF16)
    wg = jnp.concatenate([w_in[:, o_dg:o_gq], w_in[:, o_gg:o_ga]], axis=1).astype(BF16)
    wr = jnp.pad(w_in[:, o_ga:], ((0, 0), (0, LANES_V7X - rank))).astype(BF16)
    wgu = jnp.pad(w_gate_up, ((0, LANES_V7X - rank), (0, 0))).astype(BF16)

    d_half = LANES_V7X // 2
    x2 = x.reshape(b * seq, d)
    assert bs["tm_in"] == bs["tk"] and diff_w == qk_cols
    pa, pg, pr, pvt = _inproj(x2, norm_w.reshape(1, d), wa, wg, wr,
                              q_cols=qk_cols, q_scale=(d_half ** -0.5) * LOG2E, tm=bs["tm_in"])
    pa = pa.reshape(b, seq, -1)
    pg = pg.reshape(b, seq, -1)
    pr = pr.reshape(b, seq, -1)

    lam_init = _lambda_init(layer_idx)
    fqc, fkc = _alibi_feature_consts(DIFF_HEADS)
    row = lambda v: v.reshape(1, -1).astype(F32)
    ya = _diff_attention(pa, pvt, pg, row(lq1), row(lk1), row(lq2), row(lk2), row(subln_w), fqc, fkc,
                         lam_init=lam_init, tq=bs["tq"], tk=bs["tk"])
    yb = _gla(pa, pg, pr, wgu, row(b_gate), row(gla_norm_w), lb=bs["lb"])

    out = _outproj(x2, ya.reshape(b * seq, -1), yb.reshape(b * seq, -1), w_out.astype(BF16),
                   row(final_w), final_norm=final_norm, tm=bs["tm_out"])
    return out.reshape(b, seq, d)


def kernel(x, norm_w, w_in, w_gate_up, b_gate, lambda_q1, lambda_k1, lambda_q2, lambda_k2,
           diff_subln_w, gla_norm_w, w_out, final_norm_w):
    depth = norm_w.shape[0]
    h = x
    for layer in range(depth):
        h = _layer(h, norm_w[layer], w_in[layer], w_gate_up[layer], b_gate[layer],
                   lambda_q1[layer], lambda_k1[layer], lambda_q2[layer], lambda_k2[layer],
                   diff_subln_w[layer], gla_norm_w[layer], w_out[layer], final_norm_w,
                   layer, final_norm=(layer == depth - 1))
    return h
```
